```python
import math
import jax, jax.numpy as jnp
from jax import lax
import numpy as np

D_MODEL = 2048
BATCH = 8
SEQ = 4096
DEPTH = 2

LRU_WIDTH = 1024
LRU_BLOCKS = 16
LRU_BLOCK = LRU_WIDTH // LRU_BLOCKS
CONV_WIDTH = 4
LRU_C = 8.0
RET_HEADS = 8
RET_HEAD_DIM = 128
RET_WIDTH = RET_HEADS * RET_HEAD_DIM
RET_CHUNK = 128
RET_ROT_BASE = 10000.0
DIFF_HEADS = 8
DIFF_HEAD_DIM = 64
DIFF_V_DIM = 2 * DIFF_HEAD_DIM
DIFF_WIDTH = DIFF_HEADS * DIFF_V_DIM
Q_BLOCK = 128
ROPE_THETA = 500000.0
ROPE_DIM = DIFF_HEAD_DIM // 4
N_BRANCH = 3
EPS = 1e-6
IN_SPLITS = (LRU_WIDTH, LRU_WIDTH,
             RET_WIDTH, RET_WIDTH, RET_WIDTH, RET_WIDTH,
             DIFF_WIDTH, DIFF_WIDTH, DIFF_WIDTH, DIFF_WIDTH,
             N_BRANCH * D_MODEL)
D_IN = 2 * LRU_WIDTH + 4 * RET_WIDTH + 4 * DIFF_WIDTH + N_BRANCH * D_MODEL

kernel_name = "hybrid_rglru_retention_diffattn_block"


def rms_norm(x, g):
    xf = x.astype(jnp.float32)
    y = xf * lax.rsqrt(jnp.mean(xf * xf, axis=-1, keepdims=True) + EPS)
    return (y * g.astype(jnp.float32)).astype(x.dtype)


def causal_dwconv(x, w, b):
    y = lax.conv_general_dilated(
        x, w[:, None, :].astype(x.dtype), window_strides=(1,),
        padding=[(CONV_WIDTH - 1, 0)], dimension_numbers=('NWC', 'WIO', 'NWC'),
        feature_group_count=x.shape[-1])
    return y + b.astype(x.dtype)


def rg_lru(x, w_a, b_a, w_x, b_x, lam):
    Bsz, S, W = x.shape
    xb = x.reshape(Bsz, S, LRU_BLOCKS, LRU_BLOCK)
    r = jax.nn.sigmoid(jnp.einsum('bsnd,nde->bsne', xb, w_a).reshape(Bsz, S, W) + b_a)
    i = jax.nn.sigmoid(jnp.einsum('bsnd,nde->bsne', xb, w_x).reshape(Bsz, S, W) + b_x)
    log_a = (-LRU_C * r.astype(jnp.float32)) * jax.nn.softplus(-lam.astype(jnp.float32))
    a = jnp.exp(log_a)
    mult = jnp.sqrt(-jnp.expm1(2.0 * log_a))
    u = mult * (i * x).astype(jnp.float32)

    def combine(left, right):
        a1, b1 = left
        a2, b2 = right
        return a1 * a2, a2 * b1 + b2

    _, h = lax.associative_scan(combine, (a, u), axis=1)
    return h.astype(x.dtype)


def rotate_interleaved(x, cos, sin):
    x1 = x[..., 0::2]
    x2 = x[..., 1::2]
    c = cos[:, None, :].astype(x.dtype)
    s = sin[:, None, :].astype(x.dtype)
    out = jnp.stack([x1 * c - x2 * s, x1 * s + x2 * c], axis=-1)
    return out.reshape(x.shape)


def partial_rope(x, cos, sin):
    half = ROPE_DIM // 2
    x1 = x[..., :half]
    x2 = x[..., half:ROPE_DIM]
    xp = x[..., ROPE_DIM:]
    c = cos[:, None, None, :].astype(x.dtype)
    s = sin[:, None, None, :].astype(x.dtype)
    return jnp.concatenate([x1 * c - x2 * s, x1 * s + x2 * c, xp], axis=-1)


def retention(q, k, v):
    Bsz, S, H, dk = q.shape
    dv = v.shape[-1]
    C = RET_CHUNK
    N = S // C
    f32 = jnp.float32
    log_g = jnp.log1p(-jnp.exp2(-5.0 - jnp.arange(H, dtype=f32)))
    qc = q.astype(f32).reshape(Bsz, N, C, H, dk)
    kc = (k.astype(f32) * (dk ** -0.5)).reshape(Bsz, N, C, H, dk)
    vc = v.astype(f32).reshape(Bsz, N, C, H, dv)
    idx = jnp.arange(C, dtype=f32)
    rel = idx[:, None] - idx[None, :]
    intra = jnp.where(rel[None] >= 0,
                      jnp.exp(log_g[:, None, None] * jnp.maximum(rel, 0.0)[None]), 0.0)
    scores = jnp.einsum('bnihd,bnjhd->bnhij', qc, kc) * intra
    inner = jnp.einsum('bnhij,bnjhe->bnihe', scores, vc)
    k_decay = jnp.exp(log_g[:, None] * (C - 1.0 - idx)[None, :])
    kv = jnp.einsum('bnjhd,hj,bnjhe->nbhde', kc, k_decay, vc)
    chunk_decay = jnp.exp(log_g * C)[None, :, None, None]

    def step(state, kv_n):
        return state * chunk_decay + kv_n, state

    _, prev = lax.scan(step, jnp.zeros((Bsz, H, dk, dv), f32), kv)
    q_decay = jnp.exp(log_g[:, None] * (idx + 1.0)[None, :])
    cross = jnp.einsum('bnihd,hi,nbhde->bnihe', qc, q_decay, prev)
    out = (inner + cross).reshape(Bsz, S, H, dv)
    mu = jnp.mean(out, axis=-1, keepdims=True)
    var = jnp.mean(jnp.square(out - mu), axis=-1, keepdims=True)
    return (out - mu) * lax.rsqrt(var + EPS)


def diff_attention(q, k, v, lam, sub_g, lam_init):
    Bsz, S, H, _, d = q.shape
    nb = S // Q_BLOCK
    qb = q.reshape(Bsz, nb, Q_BLOCK, H, 2, d).transpose(1, 0, 3, 4, 2, 5)
    kt = k.transpose(0, 2, 3, 1, 4)
    vt = v.transpose(0, 2, 1, 3)
    key_pos = jnp.arange(S)
    scale = d ** -0.5

    def block(args):
        qblk, start = args
        s = jnp.einsum('bhcqd,bhckd->bhcqk', qblk, kt).astype(jnp.float32) * scale
        q_pos = start + jnp.arange(Q_BLOCK)
        mask = key_pos[None, :] <= q_pos[:, None]
        s = jnp.where(mask, s, -jnp.inf)
        p = jax.nn.softmax(s, axis=-1)
        w = p[:, :, 0] - lam * p[:, :, 1]
        return jnp.einsum('bhqk,bhke->bhqe', w.astype(vt.dtype), vt)

    starts = jnp.arange(nb) * Q_BLOCK
    o = lax.map(block, (qb, starts))
    o = o.transpose(1, 0, 3, 2, 4).reshape(Bsz, S, H, 2 * d)
    of = o.astype(jnp.float32)
    of = of * lax.rsqrt(jnp.mean(of * of, axis=-1, keepdims=True) + EPS) * sub_g.astype(jnp.float32)
    return (of * (1.0 - lam_init)).astype(v.dtype)


def hybrid_layer(x, layer_idx, cos_r, sin_r, cos_d, sin_d, pre_g, post_g, w_in, conv_w, conv_b,
                 lru_wa, lru_ba, lru_wx, lru_bx, lru_lambda, diff_lam, diff_sub,
                 w_br_a, w_br_b, w_br_c, w_out):
    Bsz, S, _ = x.shape
    h = rms_norm(x, pre_g)
    proj = jnp.einsum('bsd,de->bse', h, w_in)
    offsets = tuple(int(o) for o in np.cumsum(IN_SPLITS)[:-1])
    xa, ga, qr, kr, vr, gr, qd, kd, vd, gd, gm = jnp.split(proj, offsets, axis=-1)

    xa = causal_dwconv(xa, conv_w, conv_b)
    ya = rg_lru(xa, lru_wa, lru_ba, lru_wx, lru_bx, lru_lambda) * jax.nn.silu(ga)

    qr = rotate_interleaved(qr.reshape(Bsz, S, RET_HEADS, RET_HEAD_DIM), cos_r, sin_r)
    kr = rotate_interleaved(kr.reshape(Bsz, S, RET_HEADS, RET_HEAD_DIM), cos_r, sin_r)
    vr = vr.reshape(Bsz, S, RET_HEADS, RET_HEAD_DIM)
    yb = retention(qr, kr, vr).reshape(Bsz, S, RET_WIDTH).astype(x.dtype) * jax.nn.silu(gr)

    lam_init = 0.8 - 0.6 * math.exp(-0.3 * layer_idx)
    dl = diff_lam.astype(jnp.float32)
    lam = jnp.exp(jnp.sum(dl[0] * dl[1])) - jnp.exp(jnp.sum(dl[2] * dl[3])) + lam_init
    qd = partial_rope(qd.reshape(Bsz, S, DIFF_HEADS, 2, DIFF_HEAD_DIM), cos_d, sin_d)
    kd = partial_rope(kd.reshape(Bsz, S, DIFF_HEADS, 2, DIFF_HEAD_DIM), cos_d, sin_d)
    vd = vd.reshape(Bsz, S, DIFF_HEADS, DIFF_V_DIM)
    yc = diff_attention(qd, kd, vd, lam, diff_sub, lam_init).reshape(Bsz, S, DIFF_WIDTH) * jax.nn.silu(gd)

    gm = jax.nn.sigmoid(gm.reshape(Bsz, S, N_BRANCH, D_MODEL))
    m = (gm[:, :, 0] * jnp.einsum('bsw,wd->bsd', ya, w_br_a)
         + gm[:, :, 1] * jnp.einsum('bsw,wd->bsd', yb, w_br_b)
         + gm[:, :, 2] * jnp.einsum('bsw,wd->bsd', yc, w_br_c))
    o = jnp.einsum('bsd,de->bse', m, w_out)
    return x + rms_norm(o, post_g)


def setup_inputs(seed: int = 0) -> dict:
    key = jax.random.key(seed)
    ks = jax.random.split(key, 20)
    f32 = jnp.float32
    nrm = lambda k, shape, scale: jax.random.normal(k, shape, f32) * scale
    u = jax.random.uniform(ks[10], (DEPTH, LRU_WIDTH), f32, 0.9, 0.999)
    s = u ** (1.0 / LRU_C)
    lru_lambda = jnp.log(s) - jnp.log1p(-s)
    return {
        "x": nrm(ks[0], (BATCH, SEQ, D_MODEL), 1.0),
        "pre_norm": 1.0 + nrm(ks[1], (DEPTH, D_MODEL), 0.05),
        "post_norm": 1.0 + nrm(ks[2], (DEPTH, D_MODEL), 0.05),
        "w_in": nrm(ks[3], (DEPTH, D_MODEL, D_IN), D_MODEL ** -0.5),
        "conv_w": nrm(ks[4], (DEPTH, CONV_WIDTH, LRU_WIDTH), CONV_WIDTH ** -0.5),
        "conv_b": nrm(ks[5], (DEPTH, LRU_WIDTH), 0.02),
        "lru_wa": nrm(ks[6], (DEPTH, LRU_BLOCKS, LRU_BLOCK, LRU_BLOCK), LRU_BLOCK ** -0.5),
        "lru_ba": nrm(ks[7], (DEPTH, LRU_WIDTH), 0.02),
        "lru_wx": nrm(ks[8], (DEPTH, LRU_BLOCKS, LRU_BLOCK, LRU_BLOCK), LRU_BLOCK ** -0.5),
        "lru_bx": nrm(ks[9], (DEPTH, LRU_WIDTH), 0.02),
        "lru_lambda": lru_lambda,
        "diff_lambda": nrm(ks[11], (DEPTH, 4, DIFF_HEAD_DIM), 0.1),
        "diff_subln": 1.0 + nrm(ks[12], (DEPTH, DIFF_V_DIM), 0.05),
        "w_branch_a": nrm(ks[13], (DEPTH, LRU_WIDTH, D_MODEL), LRU_WIDTH ** -0.5),
        "w_branch_b": nrm(ks[14], (DEPTH, RET_WIDTH, D_MODEL), RET_WIDTH ** -0.5),
        "w_branch_c": nrm(ks[15], (DEPTH, DIFF_WIDTH, D_MODEL), DIFF_WIDTH ** -0.5),
        "w_out": nrm(ks[16], (DEPTH, D_MODEL, D_MODEL), D_MODEL ** -0.5),
    }


def reference(x, pre_norm, post_norm, w_in, conv_w, conv_b, lru_wa, lru_ba, lru_wx, lru_bx,
              lru_lambda, diff_lambda, diff_subln, w_branch_a, w_branch_b, w_branch_c, w_out):
    S = x.shape[1]
    pos = jnp.arange(S, dtype=jnp.float32)
    ret_freq = 1.0 / (RET_ROT_BASE ** jnp.linspace(0.0, 1.0, RET_HEAD_DIM // 2, dtype=jnp.float32))
    ang_r = pos[:, None] * ret_freq[None, :]
    cos_r, sin_r = jnp.cos(ang_r), jnp.sin(ang_r)
    inv_freq = ROPE_THETA ** (-jnp.arange(0, ROPE_DIM, 2, dtype=jnp.float32) / ROPE_DIM)
    ang_d = pos[:, None] * inv_freq[None, :]
    cos_d, sin_d = jnp.cos(ang_d), jnp.sin(ang_d)
    for l in range(DEPTH):
        x = hybrid_layer(x, l, cos_r, sin_r, cos_d, sin_d, pre_norm[l], post_norm[l], w_in[l],
                         conv_w[l], conv_b[l], lru_wa[l], lru_ba[l], lru_wx[l], lru_bx[l],
                         lru_lambda[l], diff_lambda[l], diff_subln[l], w_branch_a[l],
                         w_branch_b[l], w_branch_c[l], w_out[l])
    return x
```

```python
import functools
import math

import numpy as np
import jax
import jax.numpy as jnp
from jax import lax
from jax.experimental import pallas as pl
from jax.experimental.pallas import tpu as pltpu

F32 = jnp.float32
BF16 = jnp.bfloat16

D_MODEL = 2048
LRU_WIDTH = 1024
LRU_BLOCKS = 16
LRU_BLOCK = LRU_WIDTH // LRU_BLOCKS
CONV_WIDTH = 4
LRU_C = 8.0
RET_HEADS = 8
RET_HEAD_DIM = 128
RET_WIDTH = RET_HEADS * RET_HEAD_DIM
RET_CHUNK = 128
RET_ROT_BASE = 10000.0
DIFF_HEADS = 8
DIFF_HEAD_DIM = 64
DIFF_V_DIM = 2 * DIFF_HEAD_DIM
DIFF_WIDTH = DIFF_HEADS * DIFF_V_DIM
ROPE_THETA = 500000.0
ROPE_DIM = DIFF_HEAD_DIM // 4
N_BRANCH = 3
EPS = 1e-6
D_IN = 2 * LRU_WIDTH + 4 * RET_WIDTH + 4 * DIFF_WIDTH + N_BRANCH * D_MODEL

LANES = 128
SUBLANES = 8
MXU_DIM = 256
VMEM_BYTES_V7X = 64 * 1024 * 1024

SEG = 1024
SEG_XA, SEG_GA, SEG_QR, SEG_KR, SEG_VR, SEG_GR, SEG_QD, SEG_KD, SEG_VD, SEG_GD, SEG_GM = range(11)


def _sigmoid(x):
    return 1.0 / (1.0 + jnp.exp(-x))


def _vmem_limit(nbytes):
    return int(min(max(nbytes, 16 * 1024 * 1024), VMEM_BYTES_V7X - 8 * 1024 * 1024))


def _in_proj_kernel(x_ref, g_ref, w_ref, rc_ref, rs_ref, da_ref, dm_ref, dp_ref, o_ref, h_ref):
    j = pl.program_id(1)

    @pl.when(j == 0)
    def _():
        xf = x_ref[...]
        ms = jnp.mean(xf * xf, axis=-1, keepdims=True)
        h_ref[...] = ((xf * lax.rsqrt(ms + EPS)) * g_ref[...]).astype(BF16)

    acc = jnp.dot(h_ref[...], w_ref[...], preferred_element_type=F32)
    nchunk = acc.shape[1] // LANES

    is_ret = jnp.logical_or(j == SEG_QR, j == SEG_KR)
    is_dif = jnp.logical_or(j == SEG_QD, j == SEG_KD)
    is_silu = jnp.logical_or(jnp.logical_or(j == SEG_GA, j == SEG_GR), j == SEG_GD)
    is_sig = j >= SEG_GM
    is_plain = jnp.logical_or(jnp.logical_or(j == SEG_XA, j == SEG_VR), j == SEG_VD)

    @pl.when(is_plain)
    def _():
        o_ref[...] = acc.astype(BF16)

    @pl.when(is_silu)
    def _():
        o_ref[...] = (acc * _sigmoid(acc)).astype(BF16)

    @pl.when(is_sig)
    def _():
        o_ref[...] = _sigmoid(acc).astype(BF16)

    @pl.when(is_ret)
    def _():
        rc = rc_ref[...]
        rs = rs_ref[...]
        for c in range(nchunk):
            xc = acc[:, c * LANES:(c + 1) * LANES]
            o_ref[:, c * LANES:(c + 1) * LANES] = (
                xc * rc + pltpu.roll(xc, LANES // 2, 1) * rs).astype(BF16)

    @pl.when(is_dif)
    def _():
        da = da_ref[...]
        dm = dm_ref[...]
        dp = dp_ref[...]
        half = ROPE_DIM // 2
        for c in range(nchunk):
            xc = acc[:, c * LANES:(c + 1) * LANES]
            o_ref[:, c * LANES:(c + 1) * LANES] = (
                xc * da + pltpu.roll(xc, LANES - half, 1) * dm
                + pltpu.roll(xc, half, 1) * dp).astype(BF16)


def _in_proj(x2, g, w, tabs, S, bm=512, bn=SEG):
    T, D = x2.shape
    N = w.shape[1]
    assert bn == SEG and T % bm == 0 and S % bm == 0 and N % bn == 0
    nb = S // bm
    tab_spec = pl.BlockSpec((bm, LANES), lambda i, j: (i % nb, 0))
    est = (2 * bm * D * 4 + bm * D * 2 + 2 * D * bn * 2 + 2 * bm * bn * 2
           + 10 * bm * LANES * 4 + 4 * bm * bn * 4)
    return pl.pallas_call(
        _in_proj_kernel,
        grid=(T // bm, N // bn),
        in_specs=[
            pl.BlockSpec((bm, D), lambda i, j: (i, 0)),
            pl.BlockSpec((1, D), lambda i, j: (0, 0)),
            pl.BlockSpec((D, bn), lambda i, j: (0, j)),
            tab_spec, tab_spec, tab_spec, tab_spec, tab_spec,
        ],
        out_specs=pl.BlockSpec((bm, bn), lambda i, j: (i, j)),
        out_shape=jax.ShapeDtypeStruct((T, N), BF16),
        scratch_shapes=[pltpu.VMEM((bm, D), BF16)],
        compiler_params=pltpu.CompilerParams(
            dimension_semantics=("parallel", "arbitrary"),
            vmem_limit_bytes=_vmem_limit(est)),
        name="in_proj",
    )(x2, g, w, *tabs)


def _lru_kernel(xa_ref, ga_ref, cw_ref, cb_ref, wg_ref, ba_ref, bx_ref, lam_ref, o_ref,
                xext, a_s, u_s, hc):
    n = pl.program_id(1)
    tc = xa_ref.shape[0]
    pad = SUBLANES

    @pl.when(n == 0)
    def _():
        xext[0:pad, :] = jnp.zeros((pad, LRU_WIDTH), F32)
        hc[...] = jnp.zeros_like(hc)

    xext[pad:pad + tc, :] = xa_ref[...].astype(F32)
    xc = cb_ref[...] + cw_ref[CONV_WIDTH - 1:CONV_WIDTH, :] * xext[pad:pad + tc, :]
    for k in range(CONV_WIDTH - 1):
        off = pad - (CONV_WIDTH - 1) + k
        xc = xc + cw_ref[k:k + 1, :] * xext[off:off + tc, :]
    xext[0:pad, :] = xext[tc:tc + pad, :]

    xb = xc.astype(BF16)
    z = -lam_ref[...]
    sp = jnp.maximum(z, 0.0) + jnp.log1p(jnp.exp(-jnp.abs(z)))
    gw = 2 * MXU_DIM
    for c in range(LRU_WIDTH // MXU_DIM):
        sl = slice(c * MXU_DIM, (c + 1) * MXU_DIM)
        g = jnp.dot(xb[:, sl], wg_ref[c], preferred_element_type=F32)
        r = _sigmoid(g[:, 0:MXU_DIM] + ba_ref[:, sl])
        ig = _sigmoid(g[:, MXU_DIM:gw] + bx_ref[:, sl])
        log_a = (-LRU_C * r) * sp[:, sl]
        a = jnp.exp(log_a)
        a_s[:, sl] = a
        u_s[:, sl] = jnp.sqrt(-jnp.tanh(log_a) * (a * a + 1.0)) * (ig * xc[:, sl])

    def step(t, h):
        h = a_s[pl.ds(t, 1), :] * h + u_s[pl.ds(t, 1), :]
        u_s[pl.ds(t, 1), :] = h
        return h

    hc[...] = lax.fori_loop(0, tc, step, hc[...], unroll=8)
    o_ref[...] = (u_s[...] * ga_ref[...].astype(F32)).astype(BF16)


def _lru(proj, cw, cb, wg, ba, bx, lam, B, S, tc=512):
    T = proj.shape[0]
    nt = S // tc
    W = LRU_WIDTH
    vec = pl.BlockSpec((1, W), lambda b, n: (0, 0))
    est = 2 * 2 * tc * W * 2 + 2 * tc * W * 2 + 3 * tc * W * 4 + 8 * tc * W * 4
    return pl.pallas_call(
        _lru_kernel,
        grid=(B, nt),
        in_specs=[
            pl.BlockSpec((tc, W), lambda b, n: (b * nt + n, SEG_XA)),
            pl.BlockSpec((tc, W), lambda b, n: (b * nt + n, SEG_GA)),
            pl.BlockSpec((CONV_WIDTH, W), lambda b, n: (0, 0)),
            vec,
            pl.BlockSpec((W // MXU_DIM, MXU_DIM, 2 * MXU_DIM), lambda b, n: (0, 0, 0)),
            vec, vec, vec,
        ],
        out_specs=pl.BlockSpec((tc, W), lambda b, n: (b * nt + n, 0)),
        out_shape=jax.ShapeDtypeStruct((T, W), BF16),
        scratch_shapes=[
            pltpu.VMEM((tc + SUBLANES, W), F32),
            pltpu.VMEM((tc, W), F32),
            pltpu.VMEM((tc, W), F32),
            pltpu.VMEM((1, W), F32),
        ],
        compiler_params=pltpu.CompilerParams(
            dimension_semantics=("parallel", "arbitrary"),
            vmem_limit_bytes=_vmem_limit(est)),
        name="lru",
    )(proj, proj, cw, cb, wg, ba, bx, lam)


def _ret_kernel(q_ref, k_ref, v_ref, g_ref, intra_ref, kdec_ref, qdec_ref, o_ref, st_ref):
    n = pl.program_id(1)
    tc = q_ref.shape[0]
    C = RET_CHUNK
    dh = RET_HEAD_DIM

    @pl.when(n == 0)
    def _():
        st_ref[...] = jnp.zeros_like(st_ref)

    nt_dims = (((1,), (1,)), ((), ()))
    for h in range(RET_HEADS):
        sl = slice(h * dh, (h + 1) * dh)
        intra = intra_ref[h]
        kdec = kdec_ref[h]
        qdec = qdec_ref[h]
        cdec = qdec[C - 1:C, :]
        for c in range(tc // C):
            rows = slice(c * C, (c + 1) * C)
            q = q_ref[rows, sl]
            k = k_ref[rows, sl]
            v = v_ref[rows, sl]
            state = st_ref[h]
            scores = lax.dot_general(q, k, nt_dims, preferred_element_type=F32) * intra
            inner = jnp.dot(scores.astype(BF16), v, preferred_element_type=F32)
            qd = (q.astype(F32) * qdec).astype(BF16)
            cross = jnp.dot(qd, state.astype(BF16), preferred_element_type=F32)
            kd_t = (k.astype(F32) * kdec).T.astype(BF16)
            kv = jnp.dot(kd_t, v, preferred_element_type=F32)
            st_ref[h] = state * cdec + kv
            out = inner + cross
            mu = jnp.mean(out, axis=-1, keepdims=True)
            d = out - mu
            var = jnp.mean(d * d, axis=-1, keepdims=True)
            y = d * lax.rsqrt(var + EPS)
            o_ref[rows, sl] = (y * g_ref[rows, sl].astype(F32)).astype(BF16)


def _retention(proj, intra, kdec, qdec, B, S, tc=256):
    T = proj.shape[0]
    nt = S // tc
    W = RET_WIDTH
    C = RET_CHUNK

    def col(seg):
        return pl.BlockSpec((tc, W), lambda b, n: (b * nt + n, seg))

    est = 2 * 5 * tc * W * 2 + 2 * 3 * RET_HEADS * C * C * 4 + RET_HEADS * C * C * 4 + 16 * C * C * 4 * 8
    return pl.pallas_call(
        _ret_kernel,
        grid=(B, nt),
        in_specs=[
            col(SEG_QR), col(SEG_KR), col(SEG_VR), col(SEG_GR),
            pl.BlockSpec((RET_HEADS, C, C), lambda b, n: (0, 0, 0)),
            pl.BlockSpec((RET_HEADS, C, RET_HEAD_DIM), lambda b, n: (0, 0, 0)),
            pl.BlockSpec((RET_HEADS, C, RET_HEAD_DIM), lambda b, n: (0, 0, 0)),
        ],
        out_specs=pl.BlockSpec((tc, W), lambda b, n: (b * nt + n, 0)),
        out_shape=jax.ShapeDtypeStruct((T, W), BF16),
        scratch_shapes=[pltpu.VMEM((RET_HEADS, RET_HEAD_DIM, RET_HEAD_DIM), F32)],
        compiler_params=pltpu.CompilerParams(
            dimension_semantics=("parallel", "arbitrary"),
            vmem_limit_bytes=_vmem_limit(est)),
        name="retention",
    )(proj, proj, proj, proj, intra, kdec, qdec)


def _diff_kernel(q_ref, k_ref, v_ref, g_ref, dl_ref, sub_ref, o_ref,
                 m1_ref, l1_ref, a1_ref, m2_ref, l2_ref, a2_ref, *, lam_init):
    qi = pl.program_id(2)
    bq = q_ref.shape[0]
    bk = bq
    d = DIFF_HEAD_DIM
    nt_dims = (((1,), (1,)), ((), ()))

    lane = lax.broadcasted_iota(jnp.int32, (bq, 2 * d), 1)
    qs = q_ref[...].astype(F32) * (d ** -0.5)
    q1 = jnp.where(lane < d, qs, 0.0).astype(BF16)
    q2 = jnp.where(lane >= d, qs, 0.0).astype(BF16)

    for m_ref, l_ref, a_ref in ((m1_ref, l1_ref, a1_ref), (m2_ref, l2_ref, a2_ref)):
        m_ref[...] = jnp.full(m_ref.shape, -1e30, F32)
        l_ref[...] = jnp.zeros(l_ref.shape, F32)
        a_ref[...] = jnp.zeros(a_ref.shape, F32)

    def block(j, masked):
        start = pl.multiple_of(j * bk, bk)
        k = k_ref[pl.ds(start, bk), :]
        v = v_ref[pl.ds(start, bk), :]
        if masked:
            row = lax.broadcasted_iota(jnp.int32, (bq, bk), 0)
            col = lax.broadcasted_iota(jnp.int32, (bq, bk), 1)
            keep = col <= row
        for qm, m_ref, l_ref, a_ref in ((q1, m1_ref, l1_ref, a1_ref), (q2, m2_ref, l2_ref, a2_ref)):
            s = lax.dot_general(qm, k, nt_dims, preferred_element_type=F32)
            if masked:
                s = jnp.where(keep, s, -jnp.inf)
            m_prev = m_ref[...]
            m_new = jnp.maximum(m_prev, jnp.max(s, axis=-1, keepdims=True))
            alpha = jnp.exp(m_prev - m_new)
            p = jnp.exp(s - m_new[:, 0:1])
            l_ref[...] = alpha * l_ref[...] + jnp.sum(p, axis=-1, keepdims=True)
            a_ref[...] = alpha * a_ref[...] + jnp.dot(p.astype(BF16), v, preferred_element_type=F32)
            m_ref[...] = m_new

    def body(j, carry):
        block(j, False)
        return carry

    lax.fori_loop(0, qi, body, 0)
    block(qi, True)

    dl = dl_ref[...].astype(F32)
    lam = (jnp.exp(jnp.sum(dl[0:1, :] * dl[1:2, :], axis=-1, keepdims=True))
           - jnp.exp(jnp.sum(dl[2:3, :] * dl[3:4, :], axis=-1, keepdims=True)) + lam_init)
    of = a1_ref[...] / l1_ref[...] - lam * (a2_ref[...] / l2_ref[...])
    of = of * lax.rsqrt(jnp.mean(of * of, axis=-1, keepdims=True) + EPS) * sub_ref[...]
    o_ref[...] = ((of * (1.0 - lam_init)) * g_ref[...].astype(F32)).astype(BF16)


def _diff_attn(proj, dl, sub, lam_init, B, S, bq=512):
    T = proj.shape[0]
    nq = S // bq
    H = DIFF_HEADS
    dv = DIFF_V_DIM
    per_seg = SEG // dv

    def qcol(seg):
        return pl.BlockSpec((bq, dv), lambda b, h, i: (b * nq + i, seg * per_seg + h))

    def kvcol(seg):
        return pl.BlockSpec((S, dv), lambda b, h, i: (b, seg * per_seg + h))

    stat = pltpu.VMEM((bq, dv), F32)
    est = 2 * 2 * S * dv * 2 + 2 * 3 * bq * dv * 2 + 6 * bq * dv * 4 + 10 * bq * bq * 4
    return pl.pallas_call(
        functools.partial(_diff_kernel, lam_init=lam_init),
        grid=(B, H, nq),
        in_specs=[
            qcol(SEG_QD), kvcol(SEG_KD), kvcol(SEG_VD), qcol(SEG_GD),
            pl.BlockSpec((4, DIFF_HEAD_DIM), lambda b, h, i: (0, 0)),
            pl.BlockSpec((1, dv), lambda b, h, i: (0, 0)),
        ],
        out_specs=pl.BlockSpec((bq, dv), lambda b, h, i: (b * nq + i, h)),
        out_shape=jax.ShapeDtypeStruct((T, DIFF_WIDTH), BF16),
        scratch_shapes=[stat, stat, stat, stat, stat, stat],
        compiler_params=pltpu.CompilerParams(
            dimension_semantics=("parallel", "parallel", "arbitrary"),
            vmem_limit_bytes=_vmem_limit(est)),
        name="diff_attn",
    )(proj, proj, proj, proj, dl, sub)


def _merge_kernel(ya_ref, yb_ref, yc_ref, wa_ref, wb_ref, wc_ref, g0_ref, g1_ref, g2_ref, o_ref):
    m = g0_ref[...].astype(F32) * jnp.dot(ya_ref[...], wa_ref[...], preferred_element_type=F32)
    m = m + g1_ref[...].astype(F32) * jnp.dot(yb_ref[...], wb_ref[...], preferred_element_type=F32)
    m = m + g2_ref[...].astype(F32) * jnp.dot(yc_ref[...], wc_ref[...], preferred_element_type=F32)
    o_ref[...] = m.astype(BF16)


def _merge(ya, yb, yc, wa, wb, wc, proj, bm=512, bn=1024):
    T, W = ya.shape
    D = wa.shape[1]
    ybk = pl.BlockSpec((bm, W), lambda i, j: (i, 0))
    wbk = pl.BlockSpec((W, bn), lambda i, j: (0, j))
    gm0 = SEG_GM * SEG // bn

    def gate(br):
        return pl.BlockSpec((bm, bn), lambda i, j: (i, gm0 + br * (D // bn) + j))

    est = 2 * 3 * bm * W * 2 + 2 * 3 * W * bn * 2 + 2 * 4 * bm * bn * 2 + 6 * bm * bn * 4
    return pl.pallas_call(
        _merge_kernel,
        grid=(T // bm, D // bn),
        in_specs=[ybk, ybk, ybk, wbk, wbk, wbk, gate(0), gate(1), gate(2)],
        out_specs=pl.BlockSpec((bm, bn), lambda i, j: (i, j)),
        out_shape=jax.ShapeDtypeStruct((T, D), BF16),
        compiler_params=pltpu.CompilerParams(
            dimension_semantics=("parallel", "arbitrary"),
            vmem_limit_bytes=_vmem_limit(est)),
        name="merge",
    )(ya, yb, yc, wa, wb, wc, proj, proj, proj)


def _out_kernel(m_ref, w_ref, x_ref, g_ref, o_ref):
    o = jnp.dot(m_ref[...], w_ref[...], preferred_element_type=F32)
    y = o * lax.rsqrt(jnp.mean(o * o, axis=-1, keepdims=True) + EPS)
    o_ref[...] = x_ref[...] + y * g_ref[...]


def _out_proj(m, w, x2, g, bm=256):
    T, D = x2.shape
    est = 2 * bm * D * 2 + 2 * D * D * 2 + 4 * bm * D * 4 + 4 * bm * D * 4
    return pl.pallas_call(
        _out_kernel,
        grid=(T // bm,),
        in_specs=[
            pl.BlockSpec((bm, D), lambda i: (i, 0)),
            pl.BlockSpec((D, D), lambda i: (0, 0)),
            pl.BlockSpec((bm, D), lambda i: (i, 0)),
            pl.BlockSpec((1, D), lambda i: (0, 0)),
        ],
        out_specs=pl.BlockSpec((bm, D), lambda i: (i, 0)),
        out_shape=jax.ShapeDtypeStruct((T, D), F32),
        compiler_params=pltpu.CompilerParams(
            dimension_semantics=("parallel",),
            vmem_limit_bytes=_vmem_limit(est)),
        name="out_proj",
    )(m, w, x2, g)


def _rope_tables(S):
    pos = jnp.arange(S, dtype=F32)
    ret_freq = 1.0 / (RET_ROT_BASE ** jnp.linspace(0.0, 1.0, RET_HEAD_DIM // 2, dtype=F32))
    ang = pos[:, None] * ret_freq[None, :]
    c, s = jnp.cos(ang), jnp.sin(ang)
    rc = jnp.concatenate([c, c], axis=-1)
    rs = jnp.concatenate([-s, s], axis=-1)
    inv_freq = ROPE_THETA ** (-jnp.arange(0, ROPE_DIM, 2, dtype=F32) / ROPE_DIM)
    angd = pos[:, None] * inv_freq[None, :]
    cd, sd = jnp.cos(angd), jnp.sin(angd)
    half = ROPE_DIM // 2
    rest = DIFF_HEAD_DIM - ROPE_DIM
    one = jnp.ones((S, rest), F32)
    zero_r = jnp.zeros((S, rest), F32)
    zero_h = jnp.zeros((S, half), F32)
    da = jnp.concatenate([cd, cd, one], axis=-1)
    dm = jnp.concatenate([-sd, zero_h, zero_r], axis=-1)
    dp = jnp.concatenate([zero_h, sd, zero_r], axis=-1)
    rep = LANES // DIFF_HEAD_DIM
    return rc, rs, jnp.tile(da, (1, rep)), jnp.tile(dm, (1, rep)), jnp.tile(dp, (1, rep))


def _decay_tables():
    C = RET_CHUNK
    H = RET_HEADS
    scale = RET_HEAD_DIM ** -0.5
    log_g = jnp.log1p(-jnp.exp2(-5.0 - jnp.arange(H, dtype=F32)))
    idx = jnp.arange(C, dtype=F32)
    rel = idx[:, None] - idx[None, :]
    intra = jnp.where(rel[None] >= 0,
                      jnp.exp(log_g[:, None, None] * jnp.maximum(rel, 0.0)[None]), 0.0) * scale
    k_decay = jnp.exp(log_g[:, None] * (C - 1.0 - idx)[None, :]) * scale
    q_decay = jnp.exp(log_g[:, None] * (idx + 1.0)[None, :])
    kdec = jnp.broadcast_to(k_decay[:, :, None], (H, C, RET_HEAD_DIM))
    qdec = jnp.broadcast_to(q_decay[:, :, None], (H, C, RET_HEAD_DIM))
    return intra, kdec, qdec


def _prep_w_in(w):
    D = w.shape[0]

    def deint(cols):
        c = cols.reshape(D, RET_HEADS, RET_HEAD_DIM // 2, 2)
        return jnp.swapaxes(c, 2, 3).reshape(D, RET_WIDTH)

    q0, k0 = SEG_QR * SEG, SEG_KR * SEG
    parts = [w[:, :q0], deint(w[:, q0:q0 + SEG]), deint(w[:, k0:k0 + SEG]), w[:, k0 + SEG:]]
    return jnp.concatenate([p.astype(BF16) for p in parts], axis=1)


def _gate_weights(wa, wx):
    per = MXU_DIM // LRU_BLOCK
    nt = LRU_BLOCKS // per

    def dense(w):
        w4 = w.reshape(nt, per, LRU_BLOCK, LRU_BLOCK)
        eye = jnp.eye(per, dtype=w.dtype)
        d = jnp.einsum('tpde,pq->tpdqe', w4, eye)
        return d.reshape(nt, MXU_DIM, MXU_DIM)

    return jnp.concatenate([dense(wa), dense(wx)], axis=-1).astype(BF16)


def kernel(x, pre_norm, post_norm, w_in, conv_w, conv_b, lru_wa, lru_ba, lru_wx, lru_bx,
           lru_lambda, diff_lambda, diff_subln, w_branch_a, w_branch_b, w_branch_c, w_out):
    B, S, D = x.shape
    depth = w_in.shape[0]
    T = B * S
    tabs = _rope_tables(S)
    intra, kdec, qdec = _decay_tables()
    x2 = x.reshape(T, D)
    for l in range(depth):
        lam_init = 0.8 - 0.6 * math.exp(-0.3 * l)
        proj = _in_proj(x2, pre_norm[l][None, :], _prep_w_in(w_in[l]), tabs, S)
        ya = _lru(proj, conv_w[l], conv_b[l][None, :], _gate_weights(lru_wa[l], lru_wx[l]),
                  lru_ba[l][None, :], lru_bx[l][None, :], lru_lambda[l][None, :], B, S)
        yb = _retention(proj, intra, kdec, qdec, B, S)
        yc = _diff_attn(proj, diff_lambda[l], diff_subln[l][None, :], lam_init, B, S)
        m = _merge(ya, yb, yc, w_branch_a[l].astype(BF16), w_branch_b[l].astype(BF16),
                   w_branch_c[l].astype(BF16), proj)
        x2 = _out_proj(m, w_out[l].astype(BF16), x2, post_norm[l][None, :])
    return x2.reshape(B, S, D)
```

```python
import functools
import math

import numpy as np
import jax
import jax.numpy as jnp
from jax import lax
from jax.experimental import pallas as pl
from jax.experimental.pallas import tpu as pltpu

F32 = jnp.float32
BF16 = jnp.bfloat16

D_MODEL = 2048
LRU_WIDTH = 1024
LRU_BLOCKS = 16
LRU_BLOCK = LRU_WIDTH // LRU_BLOCKS
CONV_WIDTH = 4
LRU_C = 8.0
RET_HEADS = 8
RET_HEAD_DIM = 128
RET_WIDTH = RET_HEADS * RET_HEAD_DIM
RET_CHUNK = 128
RET_ROT_BASE = 10000.0
DIFF_HEADS = 8
DIFF_HEAD_DIM = 64
DIFF_V_DIM = 2 * DIFF_HEAD_DIM
DIFF_WIDTH = DIFF_HEADS * DIFF_V_DIM
ROPE_THETA = 500000.0
ROPE_DIM = DIFF_HEAD_DIM // 4
N_BRANCH = 3
EPS = 1e-6
D_IN = 2 * LRU_WIDTH + 4 * RET_WIDTH + 4 * DIFF_WIDTH + N_BRANCH * D_MODEL

LANES = 128
SUBLANES = 8
MXU_DIM = 256
VMEM_BYTES_V7X = 64 * 1024 * 1024

SEG = 1024
SEG_XA, SEG_GA, SEG_QR, SEG_KR, SEG_VR, SEG_GR, SEG_QD, SEG_KD, SEG_VD, SEG_GD, SEG_GM = range(11)


def _sigmoid(x):
    return 1.0 / (1.0 + jnp.exp(-x))


def _vmem_limit(nbytes):
    return int(min(max(nbytes, 16 * 1024 * 1024), VMEM_BYTES_V7X - 8 * 1024 * 1024))


def _in_proj_kernel(x_ref, g_ref, w_ref, rc_ref, rs_ref, da_ref, dm_ref, dp_ref, o_ref, h_ref):
    j = pl.program_id(1)

    @pl.when(j == 0)
    def _():
        xf = x_ref[...]
        ms = jnp.mean(xf * xf, axis=-1, keepdims=True)
        h_ref[...] = ((xf * lax.rsqrt(ms + EPS)) * g_ref[...]).astype(BF16)

    def project(epilogue):
        for c in range(o_ref.shape[1] // MXU_DIM):
            cols = slice(c * MXU_DIM, (c + 1) * MXU_DIM)
            acc = jnp.dot(h_ref[...], w_ref[:, cols], preferred_element_type=F32)
            o_ref[:, cols] = epilogue(acc).astype(BF16)

    def per_lane_chunk(fn):
        def epilogue(acc):
            return jnp.concatenate(
                [fn(acc[:, c * LANES:(c + 1) * LANES]) for c in range(MXU_DIM // LANES)], axis=1)
        return epilogue

    is_ret = jnp.logical_or(j == SEG_QR, j == SEG_KR)
    is_dif = jnp.logical_or(j == SEG_QD, j == SEG_KD)
    is_silu = jnp.logical_or(jnp.logical_or(j == SEG_GA, j == SEG_GR), j == SEG_GD)
    is_sig = j >= SEG_GM
    is_plain = jnp.logical_or(jnp.logical_or(j == SEG_XA, j == SEG_VR), j == SEG_VD)

    @pl.when(is_plain)
    def _():
        project(lambda acc: acc)

    @pl.when(is_silu)
    def _():
        project(lambda acc: acc * _sigmoid(acc))

    @pl.when(is_sig)
    def _():
        project(_sigmoid)

    @pl.when(is_ret)
    def _():
        rc = rc_ref[...]
        rs = rs_ref[...]
        project(per_lane_chunk(lambda xc: xc * rc + pltpu.roll(xc, LANES // 2, 1) * rs))

    @pl.when(is_dif)
    def _():
        da = da_ref[...]
        dm = dm_ref[...]
        dp = dp_ref[...]
        half = ROPE_DIM // 2
        project(per_lane_chunk(lambda xc: xc * da + pltpu.roll(xc, LANES - half, 1) * dm
                               + pltpu.roll(xc, half, 1) * dp))


def _in_proj(x2, g, w, tabs, S, bm=512, bn=SEG):
    T, D = x2.shape
    N = w.shape[1]
    assert bn == SEG and T % bm == 0 and S % bm == 0 and N % bn == 0
    nb = S // bm
    tab_spec = pl.BlockSpec((bm, LANES), lambda i, j: (i % nb, 0))
    est = (2 * bm * D * 4 + bm * D * 2 + 2 * D * bn * 2 + 2 * bm * bn * 2
           + 10 * bm * LANES * 4 + 4 * bm * bn * 4)
    return pl.pallas_call(
        _in_proj_kernel,
        grid=(T // bm, N // bn),
        in_specs=[
            pl.BlockSpec((bm, D), lambda i, j: (i, 0)),
            pl.BlockSpec((1, D), lambda i, j: (0, 0)),
            pl.BlockSpec((D, bn), lambda i, j: (0, j)),
            tab_spec, tab_spec, tab_spec, tab_spec, tab_spec,
        ],
        out_specs=pl.BlockSpec((bm, bn), lambda i, j: (i, j)),
        out_shape=jax.ShapeDtypeStruct((T, N), BF16),
        scratch_shapes=[pltpu.VMEM((bm, D), BF16)],
        compiler_params=pltpu.CompilerParams(
            dimension_semantics=("parallel", "arbitrary"),
            vmem_limit_bytes=_vmem_limit(est)),
        name="in_proj",
    )(x2, g, w, *tabs)


def _lru_kernel(xa_ref, ga_ref, cw_ref, cb_ref, wg_ref, ba_ref, bx_ref, lam_ref, o_ref,
                xext, a_s, u_s, hc):
    n = pl.program_id(1)
    tc = xa_ref.shape[0]
    pad = SUBLANES

    @pl.when(n == 0)
    def _():
        xext[0:pad, :] = jnp.zeros((pad, LRU_WIDTH), F32)
        hc[...] = jnp.zeros_like(hc)

    xext[pad:pad + tc, :] = xa_ref[...].astype(F32)
    xc = cb_ref[...] + cw_ref[CONV_WIDTH - 1:CONV_WIDTH, :] * xext[pad:pad + tc, :]
    for k in range(CONV_WIDTH - 1):
        off = pad - (CONV_WIDTH - 1) + k
        xc = xc + cw_ref[k:k + 1, :] * xext[off:off + tc, :]
    xext[0:pad, :] = xext[tc:tc + pad, :]

    xb = xc.astype(BF16)
    z = -lam_ref[...]
    sp = jnp.maximum(z, 0.0) + jnp.log1p(jnp.exp(-jnp.abs(z)))
    gw = 2 * MXU_DIM
    for c in range(LRU_WIDTH // MXU_DIM):
        sl = slice(c * MXU_DIM, (c + 1) * MXU_DIM)
        g = jnp.dot(xb[:, sl], wg_ref[c], preferred_element_type=F32)
        r = _sigmoid(g[:, 0:MXU_DIM] + ba_ref[:, sl])
        ig = _sigmoid(g[:, MXU_DIM:gw] + bx_ref[:, sl])
        log_a = (-LRU_C * r) * sp[:, sl]
        a = jnp.exp(log_a)
        a_s[:, sl] = a
        u_s[:, sl] = jnp.sqrt(-jnp.tanh(log_a) * (a * a + 1.0)) * (ig * xc[:, sl])

    def step(t, h):
        h = a_s[pl.ds(t, 1), :] * h + u_s[pl.ds(t, 1), :]
        u_s[pl.ds(t, 1), :] = h
        return h

    hc[...] = lax.fori_loop(0, tc, step, hc[...], unroll=8)
    o_ref[...] = (u_s[...] * ga_ref[...].astype(F32)).astype(BF16)


def _lru(proj, cw, cb, wg, ba, bx, lam, B, S, tc=512):
    T = proj.shape[0]
    nt = S // tc
    W = LRU_WIDTH
    vec = pl.BlockSpec((1, W), lambda b, n: (0, 0))
    est = 2 * 2 * tc * W * 2 + 2 * tc * W * 2 + 3 * tc * W * 4 + 8 * tc * W * 4
    return pl.pallas_call(
        _lru_kernel,
        grid=(B, nt),
        in_specs=[
            pl.BlockSpec((tc, W), lambda b, n: (b * nt + n, SEG_XA)),
            pl.BlockSpec((tc, W), lambda b, n: (b * nt + n, SEG_GA)),
            pl.BlockSpec((CONV_WIDTH, W), lambda b, n: (0, 0)),
            vec,
            pl.BlockSpec((W // MXU_DIM, MXU_DIM, 2 * MXU_DIM), lambda b, n: (0, 0, 0)),
            vec, vec, vec,
        ],
        out_specs=pl.BlockSpec((tc, W), lambda b, n: (b * nt + n, 0)),
        out_shape=jax.ShapeDtypeStruct((T, W), BF16),
        scratch_shapes=[
            pltpu.VMEM((tc + SUBLANES, W), F32),
            pltpu.VMEM((tc, W), F32),
            pltpu.VMEM((tc, W), F32),
            pltpu.VMEM((1, W), F32),
        ],
        compiler_params=pltpu.CompilerParams(
            dimension_semantics=("parallel", "arbitrary"),
            vmem_limit_bytes=_vmem_limit(est)),
        name="lru",
    )(proj, proj, cw, cb, wg, ba, bx, lam)


def _ret_kernel(q_ref, k_ref, v_ref, g_ref, intra_ref, kdec_ref, qdec_ref, o_ref, st_ref):
    n = pl.program_id(1)
    tc = q_ref.shape[0]
    C = RET_CHUNK
    dh = RET_HEAD_DIM

    @pl.when(n == 0)
    def _():
        st_ref[...] = jnp.zeros_like(st_ref)

    nt_dims = (((1,), (1,)), ((), ()))
    for h in range(RET_HEADS):
        sl = slice(h * dh, (h + 1) * dh)
        intra = intra_ref[h]
        kdec = kdec_ref[h]
        qdec = qdec_ref[h]
        cdec = qdec[C - 1:C, :]
        for c in range(tc // C):
            rows = slice(c * C, (c + 1) * C)
            q = q_ref[rows, sl]
            k = k_ref[rows, sl]
            v = v_ref[rows, sl]
            state = st_ref[h]
            scores = lax.dot_general(q, k, nt_dims, preferred_element_type=F32) * intra
            inner = jnp.dot(scores.astype(BF16), v, preferred_element_type=F32)
            qd = (q.astype(F32) * qdec).astype(BF16)
            cross = jnp.dot(qd, state.astype(BF16), preferred_element_type=F32)
            kd_t = (k.astype(F32) * kdec).T.astype(BF16)
            kv = jnp.dot(kd_t, v, preferred_element_type=F32)
            st_ref[h] = state * cdec + kv
            out = inner + cross
            mu = jnp.mean(out, axis=-1, keepdims=True)
            d = out - mu
            var = jnp.mean(d * d, axis=-1, keepdims=True)
            y = d * lax.rsqrt(var + EPS)
            o_ref[rows, sl] = (y * g_ref[rows, sl].astype(F32)).astype(BF16)


def _retention(proj, intra, kdec, qdec, B, S, tc=256):
    T = proj.shape[0]
    nt = S // tc
    W = RET_WIDTH
    C = RET_CHUNK

    def col(seg):
        return pl.BlockSpec((tc, W), lambda b, n: (b * nt + n, seg))

    est = 2 * 5 * tc * W * 2 + 2 * 3 * RET_HEADS * C * C * 4 + RET_HEADS * C * C * 4 + 16 * C * C * 4 * 8
    return pl.pallas_call(
        _ret_kernel,
        grid=(B, nt),
        in_specs=[
            col(SEG_QR), col(SEG_KR), col(SEG_VR), col(SEG_GR),
            pl.BlockSpec((RET_HEADS, C, C), lambda b, n: (0, 0, 0)),
            pl.BlockSpec((RET_HEADS, C, RET_HEAD_DIM), lambda b, n: (0, 0, 0)),
            pl.BlockSpec((RET_HEADS, C, RET_HEAD_DIM), lambda b, n: (0, 0, 0)),
        ],
        out_specs=pl.BlockSpec((tc, W), lambda b, n: (b * nt + n, 0)),
        out_shape=jax.ShapeDtypeStruct((T, W), BF16),
        scratch_shapes=[pltpu.VMEM((RET_HEADS, RET_HEAD_DIM, RET_HEAD_DIM), F32)],
        compiler_params=pltpu.CompilerParams(
            dimension_semantics=("parallel", "arbitrary"),
            vmem_limit_bytes=_vmem_limit(est)),
        name="retention",
    )(proj, proj, proj, proj, intra, kdec, qdec)


def _diff_kernel(q_ref, k_ref, v_ref, g_ref, dl_ref, sub_ref, o_ref, vx_ref, m_ref, a_ref, s_ref,
                 *, lam_init, bk, rs):
    qi = pl.program_id(2)
    bq = q_ref.shape[0]
    d = DIFF_HEAD_DIM
    dv = 2 * d
    nt_dims = (((1,), (1,)), ((), ()))
    nck = bk // LANES

    @pl.when(qi == 0)
    def _():
        vx_ref[:, 0:dv] = v_ref[...]
        vx_ref[:, dv:2 * dv] = jnp.ones((vx_ref.shape[0], dv), BF16)

    lane = lax.broadcasted_iota(jnp.int32, (bq, dv), 1)
    qs = q_ref[...].astype(F32) * (d ** -0.5)
    qq = jnp.concatenate([jnp.where(lane < d, qs, 0.0), jnp.where(lane >= d, qs, 0.0)],
                         axis=0).astype(BF16)

    m_ref[...] = jnp.full(m_ref.shape, -1e30, F32)
    a_ref[...] = jnp.zeros(a_ref.shape, F32)

    nsub = 2 * bq // rs

    def scores(t, slot):
        k = k_ref[pl.ds(pl.multiple_of(t * bk, bk), bk), :]
        for r in range(nsub):
            s_ref[slot, r * rs:(r + 1) * rs, :] = lax.dot_general(
                qq[r * rs:(r + 1) * rs], k, nt_dims, preferred_element_type=F32)

    def softmax_pv(t, slot, row0):
        vx = vx_ref[pl.ds(pl.multiple_of(t * bk, bk), bk), :]
        for r in range(nsub):
            rows = slice(r * rs, (r + 1) * rs)
            s = s_ref[slot, rows, :]
            if row0 is not None:
                qrow0 = row0 + (r * rs) % bq
                if qrow0 + rs - 1 < 0:
                    continue
                row = lax.broadcasted_iota(jnp.int32, (rs, bk), 0) + qrow0
                col = lax.broadcasted_iota(jnp.int32, (rs, bk), 1)
                s = jnp.where(col <= row, s, -jnp.inf)
            sc = [s[:, c * LANES:(c + 1) * LANES] for c in range(nck)]
            smax = sc[0]
            for c in range(1, nck):
                smax = jnp.maximum(smax, sc[c])
            m_prev = m_ref[rows, :]
            m_new = jnp.maximum(m_prev, jnp.max(smax, axis=-1, keepdims=True))
            alpha = jnp.exp(m_prev - m_new)
            p = jnp.concatenate([jnp.exp(x - m_new) for x in sc], axis=1).astype(BF16)
            pv = jnp.dot(p, vx, preferred_element_type=F32)
            a_ref[rows, 0:dv] = alpha * a_ref[rows, 0:dv] + pv[:, 0:dv]
            a_ref[rows, dv:2 * dv] = alpha * a_ref[rows, dv:2 * dv] + pv[:, dv:2 * dv]
            m_ref[rows, :] = m_new

    ndiag = bq // bk
    nfull = qi * ndiag

    def body(u, carry):
        t = 2 * u
        scores(t + 1, 1)
        softmax_pv(t, 0, None)
        scores(t + 2, 0)
        softmax_pv(t + 1, 1, None)
        return carry

    scores(0, 0)
    lax.fori_loop(0, nfull // 2, body, 0)
    for jj in range(ndiag):
        if jj + 1 < ndiag:
            scores(nfull + jj + 1, (jj + 1) % 2)
        softmax_pv(nfull + jj, jj % 2, -jj * bk)

    dl = dl_ref[...].astype(F32)
    lam = (jnp.exp(jnp.sum(dl[0:1, :] * dl[1:2, :], axis=-1, keepdims=True))
           - jnp.exp(jnp.sum(dl[2:3, :] * dl[3:4, :], axis=-1, keepdims=True)) + lam_init)
    o1 = a_ref[0:bq, 0:dv] / a_ref[0:bq, dv:2 * dv]
    o2 = a_ref[bq:2 * bq, 0:dv] / a_ref[bq:2 * bq, dv:2 * dv]
    of = o1 - lam * o2
    of = of * lax.rsqrt(jnp.mean(of * of, axis=-1, keepdims=True) + EPS) * sub_ref[...]
    o_ref[...] = ((of * (1.0 - lam_init)) * g_ref[...].astype(F32)).astype(BF16)


def _diff_attn(proj, dl, sub, lam_init, B, S, bq=512, bk=256, rs=128):
    T = proj.shape[0]
    nq = S // bq
    H = DIFF_HEADS
    dv = DIFF_V_DIM
    per_seg = SEG // dv
    assert bq % (2 * bk) == 0 and S % bq == 0

    def qcol(seg):
        return pl.BlockSpec((bq, dv), lambda b, h, i: (b * nq + i, seg * per_seg + h))

    def kvcol(seg):
        return pl.BlockSpec((S, dv), lambda b, h, i: (b, seg * per_seg + h))

    est = (2 * 2 * S * dv * 2 + S * 2 * dv * 2 + 2 * 3 * bq * dv * 2
           + 2 * bq * 3 * dv * 4 + 12 * bq * bk * 4)
    return pl.pallas_call(
        functools.partial(_diff_kernel, lam_init=lam_init, bk=bk, rs=rs),
        grid=(B, H, nq),
        in_specs=[
            qcol(SEG_QD), kvcol(SEG_KD), kvcol(SEG_VD), qcol(SEG_GD),
            pl.BlockSpec((4, DIFF_HEAD_DIM), lambda b, h, i: (0, 0)),
            pl.BlockSpec((1, dv), lambda b, h, i: (0, 0)),
        ],
        out_specs=pl.BlockSpec((bq, dv), lambda b, h, i: (b * nq + i, h)),
        out_shape=jax.ShapeDtypeStruct((T, DIFF_WIDTH), BF16),
        scratch_shapes=[
            pltpu.VMEM((S, 2 * dv), BF16),
            pltpu.VMEM((2 * bq, dv), F32),
            pltpu.VMEM((2 * bq, 2 * dv), F32),
            pltpu.VMEM((2, 2 * bq, bk), F32),
        ],
        compiler_params=pltpu.CompilerParams(
            dimension_semantics=("parallel", "parallel", "arbitrary"),
            vmem_limit_bytes=_vmem_limit(est)),
        name="diff_attn",
    )(proj, proj, proj, proj, dl, sub)


def _merge_kernel(ya_ref, yb_ref, yc_ref, wa_ref, wb_ref, wc_ref, g0_ref, g1_ref, g2_ref, o_ref):
    m = g0_ref[...].astype(F32) * jnp.dot(ya_ref[...], wa_ref[...], preferred_element_type=F32)
    m = m + g1_ref[...].astype(F32) * jnp.dot(yb_ref[...], wb_ref[...], preferred_element_type=F32)
    m = m + g2_ref[...].astype(F32) * jnp.dot(yc_ref[...], wc_ref[...], preferred_element_type=F32)
    o_ref[...] = m.astype(BF16)


def _merge(ya, yb, yc, wa, wb, wc, proj, bm=512, bn=1024):
    T, W = ya.shape
    D = wa.shape[1]
    ybk = pl.BlockSpec((bm, W), lambda i, j: (i, 0))
    wbk = pl.BlockSpec((W, bn), lambda i, j: (0, j))
    gm0 = SEG_GM * SEG // bn

    def gate(br):
        return pl.BlockSpec((bm, bn), lambda i, j: (i, gm0 + br * (D // bn) + j))

    est = 2 * 3 * bm * W * 2 + 2 * 3 * W * bn * 2 + 2 * 4 * bm * bn * 2 + 6 * bm * bn * 4
    return pl.pallas_call(
        _merge_kernel,
        grid=(T // bm, D // bn),
        in_specs=[ybk, ybk, ybk, wbk, wbk, wbk, gate(0), gate(1), gate(2)],
        out_specs=pl.BlockSpec((bm, bn), lambda i, j: (i, j)),
        out_shape=jax.ShapeDtypeStruct((T, D), BF16),
        compiler_params=pltpu.CompilerParams(
            dimension_semantics=("parallel", "arbitrary"),
            vmem_limit_bytes=_vmem_limit(est)),
        name="merge",
    )(ya, yb, yc, wa, wb, wc, proj, proj, proj)


def _out_kernel(m_ref, w_ref, x_ref, g_ref, o_ref):
    o = jnp.dot(m_ref[...], w_ref[...], preferred_element_type=F32)
    y = o * lax.rsqrt(jnp.mean(o * o, axis=-1, keepdims=True) + EPS)
    o_ref[...] = x_ref[...] + y * g_ref[...]


def _out_proj(m, w, x2, g, bm=256):
    T, D = x2.shape
    est = 2 * bm * D * 2 + 2 * D * D * 2 + 4 * bm * D * 4 + 4 * bm * D * 4
    return pl.pallas_call(
        _out_kernel,
        grid=(T // bm,),
        in_specs=[
            pl.BlockSpec((bm, D), lambda i: (i, 0)),
            pl.BlockSpec((D, D), lambda i: (0, 0)),
            pl.BlockSpec((bm, D), lambda i: (i, 0)),
            pl.BlockSpec((1, D), lambda i: (0, 0)),
        ],
        out_specs=pl.BlockSpec((bm, D), lambda i: (i, 0)),
        out_shape=jax.ShapeDtypeStruct((T, D), F32),
        compiler_params=pltpu.CompilerParams(
            dimension_semantics=("parallel",),
            vmem_limit_bytes=_vmem_limit(est)),
        name="out_proj",
    )(m, w, x2, g)


def _rope_tables(S):
    pos = jnp.arange(S, dtype=F32)
    ret_freq = 1.0 / (RET_ROT_BASE ** jnp.linspace(0.0, 1.0, RET_HEAD_DIM // 2, dtype=F32))
    ang = pos[:, None] * ret_freq[None, :]
    c, s = jnp.cos(ang), jnp.sin(ang)
    rc = jnp.concatenate([c, c], axis=-1)
    rs = jnp.concatenate([-s, s], axis=-1)
    inv_freq = ROPE_THETA ** (-jnp.arange(0, ROPE_DIM, 2, dtype=F32) / ROPE_DIM)
    angd = pos[:, None] * inv_freq[None, :]
    cd, sd = jnp.cos(angd), jnp.sin(angd)
    half = ROPE_DIM // 2
    rest = DIFF_HEAD_DIM - ROPE_DIM
    one = jnp.ones((S, rest), F32)
    zero_r = jnp.zeros((S, rest), F32)
    zero_h = jnp.zeros((S, half), F32)
    da = jnp.concatenate([cd, cd, one], axis=-1)
    dm = jnp.concatenate([-sd, zero_h, zero_r], axis=-1)
    dp = jnp.concatenate([zero_h, sd, zero_r], axis=-1)
    rep = LANES // DIFF_HEAD_DIM
    return rc, rs, jnp.tile(da, (1, rep)), jnp.tile(dm, (1, rep)), jnp.tile(dp, (1, rep))


def _decay_tables():
    C = RET_CHUNK
    H = RET_HEADS
    scale = RET_HEAD_DIM ** -0.5
    log_g = jnp.log1p(-jnp.exp2(-5.0 - jnp.arange(H, dtype=F32)))
    idx = jnp.arange(C, dtype=F32)
    rel = idx[:, None] - idx[None, :]
    intra = jnp.where(rel[None] >= 0,
                      jnp.exp(log_g[:, None, None] * jnp.maximum(rel, 0.0)[None]), 0.0) * scale
    k_decay = jnp.exp(log_g[:, None] * (C - 1.0 - idx)[None, :]) * scale
    q_decay = jnp.exp(log_g[:, None] * (idx + 1.0)[None, :])
    kdec = jnp.broadcast_to(k_decay[:, :, None], (H, C, RET_HEAD_DIM))
    qdec = jnp.broadcast_to(q_decay[:, :, None], (H, C, RET_HEAD_DIM))
    return intra, kdec, qdec


def _prep_w_in(w):
    D = w.shape[0]

    def deint(cols):
        c = cols.reshape(D, RET_HEADS, RET_HEAD_DIM // 2, 2)
        return jnp.swapaxes(c, 2, 3).reshape(D, RET_WIDTH)

    q0, k0 = SEG_QR * SEG, SEG_KR * SEG
    parts = [w[:, :q0], deint(w[:, q0:q0 + SEG]), deint(w[:, k0:k0 + SEG]), w[:, k0 + SEG:]]
    return jnp.concatenate([p.astype(BF16) for p in parts], axis=1)


def _gate_weights(wa, wx):
    per = MXU_DIM // LRU_BLOCK
    nt = LRU_BLOCKS // per

    def dense(w):
        w4 = w.reshape(nt, per, LRU_BLOCK, LRU_BLOCK)
        eye = jnp.eye(per, dtype=w.dtype)
        d = jnp.einsum('tpde,pq->tpdqe', w4, eye)
        return d.reshape(nt, MXU_DIM, MXU_DIM)

    return jnp.concatenate([dense(wa), dense(wx)], axis=-1).astype(BF16)


def kernel(x, pre_norm, post_norm, w_in, conv_w, conv_b, lru_wa, lru_ba, lru_wx, lru_bx,
           lru_lambda, diff_lambda, diff_subln, w_branch_a, w_branch_b, w_branch_c, w_out):
    B, S, D = x.shape
    depth = w_in.shape[0]
    T = B * S
    tabs = _rope_tables(S)
    intra, kdec, qdec = _decay_tables()
    x2 = x.reshape(T, D)
    for l in range(depth):
        lam_init = 0.8 - 0.6 * math.exp(-0.3 * l)
        proj = _in_proj(x2, pre_norm[l][None, :], _prep_w_in(w_in[l]), tabs, S)
        ya = _lru(proj, conv_w[l], conv_b[l][None, :], _gate_weights(lru_wa[l], lru_wx[l]),
                  lru_ba[l][None, :], lru_bx[l][None, :], lru_lambda[l][None, :], B, S)
        yb = _retention(proj, intra, kdec, qdec, B, S)
        yc = _diff_attn(proj, diff_lambda[l], diff_subln[l][None, :], lam_init, B, S)
        m = _merge(ya, yb, yc, w_branch_a[l].astype(BF16), w_branch_b[l].astype(BF16),
                   w_branch_c[l].astype(BF16), proj)
        x2 = _out_proj(m, w_out[l].astype(BF16), x2, post_norm[l][None, :])
    return x2.reshape(B, S, D)
```

```python
import functools
import math

import numpy as np
import jax
import jax.numpy as jnp
from jax import lax
from jax.experimental import pallas as pl
from jax.experimental.pallas import tpu as pltpu

F32 = jnp.float32
BF16 = jnp.bfloat16

D_MODEL = 2048
LRU_WIDTH = 1024
LRU_BLOCKS = 16
LRU_BLOCK = LRU_WIDTH // LRU_BLOCKS
CONV_WIDTH = 4
LRU_C = 8.0
RET_HEADS = 8
RET_HEAD_DIM = 128
RET_WIDTH = RET_HEADS * RET_HEAD_DIM
RET_CHUNK = 128
RET_ROT_BASE = 10000.0
DIFF_HEADS = 8
DIFF_HEAD_DIM = 64
DIFF_V_DIM = 2 * DIFF_HEAD_DIM
DIFF_WIDTH = DIFF_HEADS * DIFF_V_DIM
ROPE_THETA = 500000.0
ROPE_DIM = DIFF_HEAD_DIM // 4
N_BRANCH = 3
EPS = 1e-6
D_IN = 2 * LRU_WIDTH + 4 * RET_WIDTH + 4 * DIFF_WIDTH + N_BRANCH * D_MODEL

LANES = 128
SUBLANES = 8
MXU_DIM = 256
VMEM_BYTES_V7X = 64 * 1024 * 1024

SEG = 1024
NORM_ROWS = 256
PROJ_ROWS = 256
SEG_XA, SEG_GA, SEG_QR, SEG_KR, SEG_VR, SEG_GR, SEG_QD, SEG_KD, SEG_VD, SEG_GD, SEG_GM = range(11)


def _sigmoid(x):
    return 1.0 / (1.0 + jnp.exp(-x))


def _vmem_limit(nbytes):
    return int(min(max(nbytes, 16 * 1024 * 1024), VMEM_BYTES_V7X - 8 * 1024 * 1024))


def _in_proj_kernel(x_ref, g_ref, w_ref, rc_ref, rs_ref, da_ref, dm_ref, dp_ref, o_ref, h_ref):
    j = pl.program_id(1)

    @pl.when(j == 0)
    def _():
        for r in range(x_ref.shape[0] // NORM_ROWS):
            rows = slice(r * NORM_ROWS, (r + 1) * NORM_ROWS)
            xf = x_ref[rows, :]
            ms = jnp.mean(xf * xf, axis=-1, keepdims=True)
            h_ref[rows, :] = ((xf * lax.rsqrt(ms + EPS)) * g_ref[...]).astype(BF16)

    def project(epilogue):
        bm = o_ref.shape[0]
        rsub = min(bm, PROJ_ROWS)
        for c in range(o_ref.shape[1] // MXU_DIM):
            cols = slice(c * MXU_DIM, (c + 1) * MXU_DIM)
            for r in range(bm // rsub):
                rows = slice(r * rsub, (r + 1) * rsub)
                acc = jnp.dot(h_ref[rows, :], w_ref[:, cols], preferred_element_type=F32)
                o_ref[rows, cols] = epilogue(acc, rows).astype(BF16)

    def per_lane_chunk(fn):
        def epilogue(acc, rows):
            return jnp.concatenate(
                [fn(acc[:, c * LANES:(c + 1) * LANES], rows) for c in range(MXU_DIM // LANES)],
                axis=1)
        return epilogue

    is_ret = jnp.logical_or(j == SEG_QR, j == SEG_KR)
    is_dif = jnp.logical_or(j == SEG_QD, j == SEG_KD)
    is_silu = jnp.logical_or(jnp.logical_or(j == SEG_GA, j == SEG_GR), j == SEG_GD)
    is_sig = j >= SEG_GM
    is_plain = jnp.logical_or(jnp.logical_or(j == SEG_XA, j == SEG_VR), j == SEG_VD)

    @pl.when(is_plain)
    def _():
        project(lambda acc, rows: acc)

    @pl.when(is_silu)
    def _():
        project(lambda acc, rows: acc * _sigmoid(acc))

    @pl.when(is_sig)
    def _():
        project(lambda acc, rows: _sigmoid(acc))

    @pl.when(is_ret)
    def _():
        project(per_lane_chunk(
            lambda xc, rows: xc * rc_ref[rows, :] + pltpu.roll(xc, LANES // 2, 1) * rs_ref[rows, :]))

    @pl.when(is_dif)
    def _():
        half = ROPE_DIM // 2
        project(per_lane_chunk(
            lambda xc, rows: xc * da_ref[rows, :] + pltpu.roll(xc, LANES - half, 1) * dm_ref[rows, :]
            + pltpu.roll(xc, half, 1) * dp_ref[rows, :]))


def _in_proj(x2, g, w, tabs, S, bm=1024, bn=SEG):
    T, D = x2.shape
    N = w.shape[1]
    assert bn == SEG and T % bm == 0 and S % bm == 0 and N % bn == 0
    nb = S // bm
    tab_spec = pl.BlockSpec((bm, LANES), lambda i, j: (i % nb, 0))
    est = (bm * D * 4 + bm * D * 2 + 2 * D * bn * 2 + 2 * bm * bn * 2
           + 10 * bm * LANES * 4 + 8 * bm * MXU_DIM * 4 + 4 * NORM_ROWS * D * 4)
    return pl.pallas_call(
        _in_proj_kernel,
        grid=(T // bm, N // bn),
        in_specs=[
            pl.BlockSpec((bm, D), lambda i, j: (i, 0), pipeline_mode=pl.Buffered(1)),
            pl.BlockSpec((1, D), lambda i, j: (0, 0)),
            pl.BlockSpec((D, bn), lambda i, j: (0, j)),
            tab_spec, tab_spec, tab_spec, tab_spec, tab_spec,
        ],
        out_specs=pl.BlockSpec((bm, bn), lambda i, j: (i, j)),
        out_shape=jax.ShapeDtypeStruct((T, N), BF16),
        scratch_shapes=[pltpu.VMEM((bm, D), BF16)],
        compiler_params=pltpu.CompilerParams(
            dimension_semantics=("parallel", "arbitrary"),
            vmem_limit_bytes=_vmem_limit(est)),
        name="in_proj",
    )(x2, g, w, *tabs)


def _lru_kernel(xa_ref, ga_ref, cw_ref, cb_ref, wg_ref, ba_ref, bx_ref, lam_ref, o_ref,
                xext, a_s, u_s, hc):
    n = pl.program_id(1)
    tc = xa_ref.shape[0]
    pad = SUBLANES

    @pl.when(n == 0)
    def _():
        xext[0:pad, :] = jnp.zeros((pad, LRU_WIDTH), F32)
        hc[...] = jnp.zeros_like(hc)

    xext[pad:pad + tc, :] = xa_ref[...].astype(F32)
    xc = cb_ref[...] + cw_ref[CONV_WIDTH - 1:CONV_WIDTH, :] * xext[pad:pad + tc, :]
    for k in range(CONV_WIDTH - 1):
        off = pad - (CONV_WIDTH - 1) + k
        xc = xc + cw_ref[k:k + 1, :] * xext[off:off + tc, :]
    xext[0:pad, :] = xext[tc:tc + pad, :]

    xb = xc.astype(BF16)
    z = -lam_ref[...]
    sp = jnp.maximum(z, 0.0) + jnp.log1p(jnp.exp(-jnp.abs(z)))
    gw = 2 * MXU_DIM
    for c in range(LRU_WIDTH // MXU_DIM):
        sl = slice(c * MXU_DIM, (c + 1) * MXU_DIM)
        g = jnp.dot(xb[:, sl], wg_ref[c], preferred_element_type=F32)
        r = _sigmoid(g[:, 0:MXU_DIM] + ba_ref[:, sl])
        ig = _sigmoid(g[:, MXU_DIM:gw] + bx_ref[:, sl])
        log_a = (-LRU_C * r) * sp[:, sl]
        a = jnp.exp(log_a)
        a_s[:, sl] = a
        u_s[:, sl] = jnp.sqrt(-jnp.tanh(log_a) * (a * a + 1.0)) * (ig * xc[:, sl])

    def step(t, h):
        h = a_s[pl.ds(t, 1), :] * h + u_s[pl.ds(t, 1), :]
        u_s[pl.ds(t, 1), :] = h
        return h

    hc[...] = lax.fori_loop(0, tc, step, hc[...], unroll=8)
    o_ref[...] = (u_s[...] * ga_ref[...].astype(F32)).astype(BF16)


def _lru(proj, cw, cb, wg, ba, bx, lam, B, S, tc=512):
    T = proj.shape[0]
    nt = S // tc
    W = LRU_WIDTH
    vec = pl.BlockSpec((1, W), lambda b, n: (0, 0))
    est = 2 * 2 * tc * W * 2 + 2 * tc * W * 2 + 3 * tc * W * 4 + 8 * tc * W * 4
    return pl.pallas_call(
        _lru_kernel,
        grid=(B, nt),
        in_specs=[
            pl.BlockSpec((tc, W), lambda b, n: (b * nt + n, SEG_XA)),
            pl.BlockSpec((tc, W), lambda b, n: (b * nt + n, SEG_GA)),
            pl.BlockSpec((CONV_WIDTH, W), lambda b, n: (0, 0)),
            vec,
            pl.BlockSpec((W // MXU_DIM, MXU_DIM, 2 * MXU_DIM), lambda b, n: (0, 0, 0)),
            vec, vec, vec,
        ],
        out_specs=pl.BlockSpec((tc, W), lambda b, n: (b * nt + n, 0)),
        out_shape=jax.ShapeDtypeStruct((T, W), BF16),
        scratch_shapes=[
            pltpu.VMEM((tc + SUBLANES, W), F32),
            pltpu.VMEM((tc, W), F32),
            pltpu.VMEM((tc, W), F32),
            pltpu.VMEM((1, W), F32),
        ],
        compiler_params=pltpu.CompilerParams(
            dimension_semantics=("parallel", "arbitrary"),
            vmem_limit_bytes=_vmem_limit(est)),
        name="lru",
    )(proj, proj, cw, cb, wg, ba, bx, lam)


def _ret_kernel(q_ref, k_ref, v_ref, g_ref, intra_ref, kdec_ref, qdec_ref, o_ref, st_ref):
    n = pl.program_id(1)
    tc = q_ref.shape[0]
    C = RET_CHUNK
    dh = RET_HEAD_DIM

    @pl.when(n == 0)
    def _():
        st_ref[...] = jnp.zeros_like(st_ref)

    nt_dims = (((1,), (1,)), ((), ()))
    for h in range(RET_HEADS):
        sl = slice(h * dh, (h + 1) * dh)
        intra = intra_ref[h]
        kdec = kdec_ref[h]
        qdec = qdec_ref[h]
        cdec = qdec[C - 1:C, :]
        for c in range(tc // C):
            rows = slice(c * C, (c + 1) * C)
            q = q_ref[rows, sl]
            k = k_ref[rows, sl]
            v = v_ref[rows, sl]
            state = st_ref[h]
            scores = lax.dot_general(q, k, nt_dims, preferred_element_type=F32) * intra
            inner = jnp.dot(scores.astype(BF16), v, preferred_element_type=F32)
            qd = (q.astype(F32) * qdec).astype(BF16)
            cross = jnp.dot(qd, state.astype(BF16), preferred_element_type=F32)
            kd_t = (k.astype(F32) * kdec).T.astype(BF16)
            kv = jnp.dot(kd_t, v, preferred_element_type=F32)
            st_ref[h] = state * cdec + kv
            out = inner + cross
            mu = jnp.mean(out, axis=-1, keepdims=True)
            d = out - mu
            var = jnp.mean(d * d, axis=-1, keepdims=True)
            y = d * lax.rsqrt(var + EPS)
            o_ref[rows, sl] = (y * g_ref[rows, sl].astype(F32)).astype(BF16)


def _retention(proj, intra, kdec, qdec, B, S, tc=256):
    T = proj.shape[0]
    nt = S // tc
    W = RET_WIDTH
    C = RET_CHUNK

    def col(seg):
        return pl.BlockSpec((tc, W), lambda b, n: (b * nt + n, seg))

    est = 2 * 5 * tc * W * 2 + 2 * 3 * RET_HEADS * C * C * 4 + RET_HEADS * C * C * 4 + 16 * C * C * 4 * 8
    return pl.pallas_call(
        _ret_kernel,
        grid=(B, nt),
        in_specs=[
            col(SEG_QR), col(SEG_KR), col(SEG_VR), col(SEG_GR),
            pl.BlockSpec((RET_HEADS, C, C), lambda b, n: (0, 0, 0)),
            pl.BlockSpec((RET_HEADS, C, RET_HEAD_DIM), lambda b, n: (0, 0, 0)),
            pl.BlockSpec((RET_HEADS, C, RET_HEAD_DIM), lambda b, n: (0, 0, 0)),
        ],
        out_specs=pl.BlockSpec((tc, W), lambda b, n: (b * nt + n, 0)),
        out_shape=jax.ShapeDtypeStruct((T, W), BF16),
        scratch_shapes=[pltpu.VMEM((RET_HEADS, RET_HEAD_DIM, RET_HEAD_DIM), F32)],
        compiler_params=pltpu.CompilerParams(
            dimension_semantics=("parallel", "arbitrary"),
            vmem_limit_bytes=_vmem_limit(est)),
        name="retention",
    )(proj, proj, proj, proj, intra, kdec, qdec)


def _diff_kernel(q_ref, k_ref, v_ref, g_ref, dl_ref, sub_ref, o_ref, vx_ref, m_ref, a_ref, s_ref,
                 *, lam_init, bk, rs):
    qi = pl.program_id(2)
    bq = q_ref.shape[0]
    d = DIFF_HEAD_DIM
    dv = 2 * d
    nt_dims = (((1,), (1,)), ((), ()))
    nck = bk // LANES

    @pl.when(qi == 0)
    def _():
        vx_ref[:, 0:dv] = v_ref[...]
        vx_ref[:, dv:2 * dv] = jnp.ones((vx_ref.shape[0], dv), BF16)

    lane = lax.broadcasted_iota(jnp.int32, (bq, dv), 1)
    qs = q_ref[...].astype(F32) * (d ** -0.5 * math.log2(math.e))
    qq = jnp.concatenate([jnp.where(lane < d, qs, 0.0), jnp.where(lane >= d, qs, 0.0)],
                         axis=0).astype(BF16)

    m_ref[...] = jnp.full(m_ref.shape, -1e30, F32)
    a_ref[...] = jnp.zeros(a_ref.shape, F32)

    nsub = 2 * bq // rs

    def visible(r, row0):
        return row0 is None or row0 + (r * rs) % bq + rs - 1 >= 0

    def scores(t, slot, row0=None):
        k = k_ref[pl.ds(pl.multiple_of(t * bk, bk), bk), :]
        for r in range(nsub):
            if not visible(r, row0):
                continue
            s_ref[slot, r * rs:(r + 1) * rs, :] = lax.dot_general(
                qq[r * rs:(r + 1) * rs], k, nt_dims, preferred_element_type=F32)

    def softmax_pv(t, slot, row0):
        vx = vx_ref[pl.ds(pl.multiple_of(t * bk, bk), bk), :]
        for r in range(nsub):
            rows = slice(r * rs, (r + 1) * rs)
            s = s_ref[slot, rows, :]
            if not visible(r, row0):
                continue
            if row0 is not None:
                qrow0 = row0 + (r * rs) % bq
                row = lax.broadcasted_iota(jnp.int32, (rs, bk), 0) + qrow0
                col = lax.broadcasted_iota(jnp.int32, (rs, bk), 1)
                s = jnp.where(col <= row, s, -jnp.inf)
            sc = [s[:, c * LANES:(c + 1) * LANES] for c in range(nck)]
            smax = sc[0]
            for c in range(1, nck):
                smax = jnp.maximum(smax, sc[c])
            m_prev = m_ref[rows, :]
            m_new = jnp.maximum(m_prev, jnp.max(smax, axis=-1, keepdims=True))
            alpha = jnp.exp2(m_prev - m_new)
            p = jnp.concatenate([jnp.exp2(x - m_new) for x in sc], axis=1).astype(BF16)
            pv = jnp.dot(p, vx, preferred_element_type=F32)
            a_ref[rows, 0:dv] = alpha * a_ref[rows, 0:dv] + pv[:, 0:dv]
            a_ref[rows, dv:2 * dv] = alpha * a_ref[rows, dv:2 * dv] + pv[:, dv:2 * dv]
            m_ref[rows, :] = m_new

    ndiag = bq // bk
    nfull = qi * ndiag

    def body(u, carry):
        t = 2 * u
        scores(t + 1, 1)
        softmax_pv(t, 0, None)
        scores(t + 2, 0)
        softmax_pv(t + 1, 1, None)
        return carry

    scores(0, 0)
    lax.fori_loop(0, nfull // 2, body, 0)
    for jj in range(ndiag):
        if jj + 1 < ndiag:
            scores(nfull + jj + 1, (jj + 1) % 2, -(jj + 1) * bk)
        softmax_pv(nfull + jj, jj % 2, -jj * bk)

    dl = dl_ref[...].astype(F32)
    lam = (jnp.exp(jnp.sum(dl[0:1, :] * dl[1:2, :], axis=-1, keepdims=True))
           - jnp.exp(jnp.sum(dl[2:3, :] * dl[3:4, :], axis=-1, keepdims=True)) + lam_init)
    o1 = a_ref[0:bq, 0:dv] / a_ref[0:bq, dv:2 * dv]
    o2 = a_ref[bq:2 * bq, 0:dv] / a_ref[bq:2 * bq, dv:2 * dv]
    of = o1 - lam * o2
    of = of * lax.rsqrt(jnp.mean(of * of, axis=-1, keepdims=True) + EPS) * sub_ref[...]
    o_ref[...] = ((of * (1.0 - lam_init)) * g_ref[...].astype(F32)).astype(BF16)


def _diff_attn(proj, dl, sub, lam_init, B, S, bq=1024, bk=512, rs=256):
    T = proj.shape[0]
    nq = S // bq
    H = DIFF_HEADS
    dv = DIFF_V_DIM
    per_seg = SEG // dv
    assert bq % (2 * bk) == 0 and S % bq == 0

    def qcol(seg):
        return pl.BlockSpec((bq, dv), lambda b, h, i: (b * nq + i, seg * per_seg + h))

    def kvcol(seg):
        return pl.BlockSpec((S, dv), lambda b, h, i: (b, seg * per_seg + h))

    est = (2 * 2 * S * dv * 2 + S * 2 * dv * 2 + 2 * 3 * bq * dv * 2
           + 2 * bq * 3 * dv * 4 + 12 * bq * bk * 4)
    return pl.pallas_call(
        functools.partial(_diff_kernel, lam_init=lam_init, bk=bk, rs=rs),
        grid=(B, H, nq),
        in_specs=[
            qcol(SEG_QD), kvcol(SEG_KD), kvcol(SEG_VD), qcol(SEG_GD),
            pl.BlockSpec((4, DIFF_HEAD_DIM), lambda b, h, i: (0, 0)),
            pl.BlockSpec((1, dv), lambda b, h, i: (0, 0)),
        ],
        out_specs=pl.BlockSpec((bq, dv), lambda b, h, i: (b * nq + i, h)),
        out_shape=jax.ShapeDtypeStruct((T, DIFF_WIDTH), BF16),
        scratch_shapes=[
            pltpu.VMEM((S, 2 * dv), BF16),
            pltpu.VMEM((2 * bq, dv), F32),
            pltpu.VMEM((2 * bq, 2 * dv), F32),
            pltpu.VMEM((2, 2 * bq, bk), F32),
        ],
        compiler_params=pltpu.CompilerParams(
            dimension_semantics=("parallel", "parallel", "arbitrary"),
            vmem_limit_bytes=_vmem_limit(est)),
        name="diff_attn",
    )(proj, proj, proj, proj, dl, sub)


def _merge_kernel(ya_ref, yb_ref, yc_ref, wa_ref, wb_ref, wc_ref, g0_ref, g1_ref, g2_ref, o_ref):
    m = g0_ref[...].astype(F32) * jnp.dot(ya_ref[...], wa_ref[...], preferred_element_type=F32)
    m = m + g1_ref[...].astype(F32) * jnp.dot(yb_ref[...], wb_ref[...], preferred_element_type=F32)
    m = m + g2_ref[...].astype(F32) * jnp.dot(yc_ref[...], wc_ref[...], preferred_element_type=F32)
    o_ref[...] = m.astype(BF16)


def _merge(ya, yb, yc, wa, wb, wc, proj, bm=512, bn=1024):
    T, W = ya.shape
    D = wa.shape[1]
    ybk = pl.BlockSpec((bm, W), lambda i, j: (i, 0))
    wbk = pl.BlockSpec((W, bn), lambda i, j: (0, j))
    gm0 = SEG_GM * SEG // bn

    def gate(br):
        return pl.BlockSpec((bm, bn), lambda i, j: (i, gm0 + br * (D // bn) + j))

    est = 2 * 3 * bm * W * 2 + 2 * 3 * W * bn * 2 + 2 * 4 * bm * bn * 2 + 6 * bm * bn * 4
    return pl.pallas_call(
        _merge_kernel,
        grid=(T // bm, D // bn),
        in_specs=[ybk, ybk, ybk, wbk, wbk, wbk, gate(0), gate(1), gate(2)],
        out_specs=pl.BlockSpec((bm, bn), lambda i, j: (i, j)),
        out_shape=jax.ShapeDtypeStruct((T, D), BF16),
        compiler_params=pltpu.CompilerParams(
            dimension_semantics=("parallel", "arbitrary"),
            vmem_limit_bytes=_vmem_limit(est)),
        name="merge",
    )(ya, yb, yc, wa, wb, wc, proj, proj, proj)


def _out_kernel(m_ref, w_ref, x_ref, g_ref, o_ref):
    o = jnp.dot(m_ref[...], w_ref[...], preferred_element_type=F32)
    y = o * lax.rsqrt(jnp.mean(o * o, axis=-1, keepdims=True) + EPS)
    o_ref[...] = x_ref[...] + y * g_ref[...]


def _out_proj(m, w, x2, g, bm=256):
    T, D = x2.shape
    est = 2 * bm * D * 2 + 2 * D * D * 2 + 4 * bm * D * 4 + 4 * bm * D * 4
    return pl.pallas_call(
        _out_kernel,
        grid=(T // bm,),
        in_specs=[
            pl.BlockSpec((bm, D), lambda i: (i, 0)),
            pl.BlockSpec((D, D), lambda i: (0, 0)),
            pl.BlockSpec((bm, D), lambda i: (i, 0)),
            pl.BlockSpec((1, D), lambda i: (0, 0)),
        ],
        out_specs=pl.BlockSpec((bm, D), lambda i: (i, 0)),
        out_shape=jax.ShapeDtypeStruct((T, D), F32),
        compiler_params=pltpu.CompilerParams(
            dimension_semantics=("parallel",),
            vmem_limit_bytes=_vmem_limit(est)),
        name="out_proj",
    )(m, w, x2, g)


def _rope_tables(S):
    pos = jnp.arange(S, dtype=F32)
    ret_freq = 1.0 / (RET_ROT_BASE ** jnp.linspace(0.0, 1.0, RET_HEAD_DIM // 2, dtype=F32))
    ang = pos[:, None] * ret_freq[None, :]
    c, s = jnp.cos(ang), jnp.sin(ang)
    rc = jnp.concatenate([c, c], axis=-1)
    rs = jnp.concatenate([-s, s], axis=-1)
    inv_freq = ROPE_THETA ** (-jnp.arange(0, ROPE_DIM, 2, dtype=F32) / ROPE_DIM)
    angd = pos[:, None] * inv_freq[None, :]
    cd, sd = jnp.cos(angd), jnp.sin(angd)
    half = ROPE_DIM // 2
    rest = DIFF_HEAD_DIM - ROPE_DIM
    one = jnp.ones((S, rest), F32)
    zero_r = jnp.zeros((S, rest), F32)
    zero_h = jnp.zeros((S, half), F32)
    da = jnp.concatenate([cd, cd, one], axis=-1)
    dm = jnp.concatenate([-sd, zero_h, zero_r], axis=-1)
    dp = jnp.concatenate([zero_h, sd, zero_r], axis=-1)
    rep = LANES // DIFF_HEAD_DIM
    return rc, rs, jnp.tile(da, (1, rep)), jnp.tile(dm, (1, rep)), jnp.tile(dp, (1, rep))


def _decay_tables():
    C = RET_CHUNK
    H = RET_HEADS
    scale = RET_HEAD_DIM ** -0.5
    log_g = jnp.log1p(-jnp.exp2(-5.0 - jnp.arange(H, dtype=F32)))
    idx = jnp.arange(C, dtype=F32)
    rel = idx[:, None] - idx[None, :]
    intra = jnp.where(rel[None] >= 0,
                      jnp.exp(log_g[:, None, None] * jnp.maximum(rel, 0.0)[None]), 0.0) * scale
    k_decay = jnp.exp(log_g[:, None] * (C - 1.0 - idx)[None, :]) * scale
    q_decay = jnp.exp(log_g[:, None] * (idx + 1.0)[None, :])
    kdec = jnp.broadcast_to(k_decay[:, :, None], (H, C, RET_HEAD_DIM))
    qdec = jnp.broadcast_to(q_decay[:, :, None], (H, C, RET_HEAD_DIM))
    return intra, kdec, qdec


def _prep_w_in(w):
    D = w.shape[0]

    def deint(cols):
        c = cols.reshape(D, RET_HEADS, RET_HEAD_DIM // 2, 2)
        return jnp.swapaxes(c, 2, 3).reshape(D, RET_WIDTH)

    q0, k0 = SEG_QR * SEG, SEG_KR * SEG
    parts = [w[:, :q0], deint(w[:, q0:q0 + SEG]), deint(w[:, k0:k0 + SEG]), w[:, k0 + SEG:]]
    return jnp.concatenate([p.astype(BF16) for p in parts], axis=1)


def _gate_weights(wa, wx):
    per = MXU_DIM // LRU_BLOCK
    nt = LRU_BLOCKS // per

    def dense(w):
        w4 = w.reshape(nt, per, LRU_BLOCK, LRU_BLOCK)
        eye = jnp.eye(per, dtype=w.dtype)
        d = jnp.einsum('tpde,pq->tpdqe', w4, eye)
        return d.reshape(nt, MXU_DIM, MXU_DIM)

    return jnp.concatenate([dense(wa), dense(wx)], axis=-1).astype(BF16)


def kernel(x, pre_norm, post_norm, w_in, conv_w, conv_b, lru_wa, lru_ba, lru_wx, lru_bx,
           lru_lambda, diff_lambda, diff_subln, w_branch_a, w_branch_b, w_branch_c, w_out):
    B, S, D = x.shape
    depth = w_in.shape[0]
    T = B * S
    tabs = _rope_tables(S)
    intra, kdec, qdec = _decay_tables()
    x2 = x.reshape(T, D)
    for l in range(depth):
        lam_init = 0.8 - 0.6 * math.exp(-0.3 * l)
        proj = _in_proj(x2, pre_norm[l][None, :], _prep_w_in(w_in[l]), tabs, S)
        ya = _lru(proj, conv_w[l], conv_b[l][None, :], _gate_weights(lru_wa[l], lru_wx[l]),
                  lru_ba[l][None, :], lru_bx[l][None, :], lru_lambda[l][None, :], B, S)
        yb = _retention(proj, intra, kdec, qdec, B, S)
        yc = _diff_attn(proj, diff_lambda[l], diff_subln[l][None, :], lam_init, B, S)
        m = _merge(ya, yb, yc, w_branch_a[l].astype(BF16), w_branch_b[l].astype(BF16),
                   w_branch_c[l].astype(BF16), proj)
        x2 = _out_proj(m, w_out[l].astype(BF16), x2, post_norm[l][None, :])
    return x2.reshape(B, S, D)
```

```python
import functools
import math

import numpy as np
import jax
import jax.numpy as jnp
from jax import lax
from jax.experimental import pallas as pl
from jax.experimental.pallas import tpu as pltpu

F32 = jnp.float32
BF16 = jnp.bfloat16

D_MODEL = 2048
LRU_WIDTH = 1024
LRU_BLOCKS = 16
LRU_BLOCK = LRU_WIDTH // LRU_BLOCKS
CONV_WIDTH = 4
LRU_C = 8.0
RET_HEADS = 8
RET_HEAD_DIM = 128
RET_WIDTH = RET_HEADS * RET_HEAD_DIM
RET_CHUNK = 128
RET_ROT_BASE = 10000.0
DIFF_HEADS = 8
DIFF_HEAD_DIM = 64
DIFF_V_DIM = 2 * DIFF_HEAD_DIM
DIFF_WIDTH = DIFF_HEADS * DIFF_V_DIM
ROPE_THETA = 500000.0
ROPE_DIM = DIFF_HEAD_DIM // 4
N_BRANCH = 3
EPS = 1e-6
D_IN = 2 * LRU_WIDTH + 4 * RET_WIDTH + 4 * DIFF_WIDTH + N_BRANCH * D_MODEL

LANES = 128
SUBLANES = 8
MXU_DIM = 256
VMEM_BYTES_V7X = 64 * 1024 * 1024

SEG = 1024
NORM_ROWS = 256
PROJ_ROWS = 256
SEG_XA, SEG_GA, SEG_QR, SEG_KR, SEG_VR, SEG_GR, SEG_QD, SEG_KD, SEG_VD, SEG_GD, SEG_GM = range(11)


def _sigmoid(x):
    return 1.0 / (1.0 + jnp.exp(-x))


def _vmem_limit(nbytes):
    return int(min(max(nbytes, 16 * 1024 * 1024), VMEM_BYTES_V7X - 8 * 1024 * 1024))


def _in_proj_kernel(x_ref, g_ref, w_ref, rc_ref, rm_ref, rp_ref, da_ref, dm_ref, dp_ref, o_ref, h_ref):
    j = pl.program_id(1)

    @pl.when(j == 0)
    def _():
        for r in range(x_ref.shape[0] // NORM_ROWS):
            rows = slice(r * NORM_ROWS, (r + 1) * NORM_ROWS)
            xf = x_ref[rows, :]
            ms = jnp.mean(xf * xf, axis=-1, keepdims=True)
            h_ref[rows, :] = ((xf * lax.rsqrt(ms + EPS)) * g_ref[...]).astype(BF16)

    def project(epilogue):
        bm = o_ref.shape[0]
        rsub = min(bm, PROJ_ROWS)
        for c in range(o_ref.shape[1] // MXU_DIM):
            cols = slice(c * MXU_DIM, (c + 1) * MXU_DIM)
            for r in range(bm // rsub):
                rows = slice(r * rsub, (r + 1) * rsub)
                acc = jnp.dot(h_ref[rows, :], w_ref[:, cols], preferred_element_type=F32)
                o_ref[rows, cols] = epilogue(acc, rows).astype(BF16)

    def per_lane_chunk(fn):
        def epilogue(acc, rows):
            return jnp.concatenate(
                [fn(acc[:, c * LANES:(c + 1) * LANES], rows) for c in range(MXU_DIM // LANES)],
                axis=1)
        return epilogue

    is_ret = jnp.logical_or(j == SEG_QR, j == SEG_KR)
    is_dif = jnp.logical_or(j == SEG_QD, j == SEG_KD)
    is_silu = jnp.logical_or(jnp.logical_or(j == SEG_GA, j == SEG_GR), j == SEG_GD)
    is_sig = j >= SEG_GM
    is_plain = jnp.logical_or(jnp.logical_or(j == SEG_XA, j == SEG_VR), j == SEG_VD)

    @pl.when(is_plain)
    def _():
        project(lambda acc, rows: acc)

    @pl.when(is_silu)
    def _():
        project(lambda acc, rows: acc * _sigmoid(acc))

    @pl.when(is_sig)
    def _():
        project(lambda acc, rows: _sigmoid(acc))

    @pl.when(is_ret)
    def _():
        project(per_lane_chunk(
            lambda xc, rows: xc * rc_ref[rows, :] + pltpu.roll(xc, LANES - 1, 1) * rm_ref[rows, :]
            + pltpu.roll(xc, 1, 1) * rp_ref[rows, :]))

    @pl.when(is_dif)
    def _():
        half = ROPE_DIM // 2
        project(per_lane_chunk(
            lambda xc, rows: xc * da_ref[rows, :] + pltpu.roll(xc, LANES - half, 1) * dm_ref[rows, :]
            + pltpu.roll(xc, half, 1) * dp_ref[rows, :]))


def _in_proj(x2, g, w, tabs, S, bm=1024, bn=SEG):
    T, D = x2.shape
    N = w.shape[1]
    assert bn == SEG and T % bm == 0 and S % bm == 0 and N % bn == 0
    nb = S // bm
    tab_spec = pl.BlockSpec((bm, LANES), lambda i, j: (i % nb, 0))
    est = (bm * D * 4 + bm * D * 2 + 2 * D * bn * 2 + 2 * bm * bn * 2
           + 12 * bm * LANES * 4 + 8 * bm * MXU_DIM * 4 + 4 * NORM_ROWS * D * 4)
    return pl.pallas_call(
        _in_proj_kernel,
        grid=(T // bm, N // bn),
        in_specs=[
            pl.BlockSpec((bm, D), lambda i, j: (i, 0), pipeline_mode=pl.Buffered(1)),
            pl.BlockSpec((1, D), lambda i, j: (0, 0)),
            pl.BlockSpec((D, bn), lambda i, j: (0, j)),
            tab_spec, tab_spec, tab_spec, tab_spec, tab_spec, tab_spec,
        ],
        out_specs=pl.BlockSpec((bm, bn), lambda i, j: (i, j)),
        out_shape=jax.ShapeDtypeStruct((T, N), BF16),
        scratch_shapes=[pltpu.VMEM((bm, D), BF16)],
        compiler_params=pltpu.CompilerParams(
            dimension_semantics=("parallel", "arbitrary"),
            vmem_limit_bytes=_vmem_limit(est)),
        name="in_proj",
    )(x2, g, w, *tabs)


def _lru_kernel(xa_ref, ga_ref, cw_ref, cb_ref, wg_ref, ba_ref, bx_ref, lam_ref, o_ref,
                xext, a_s, u_s, hc):
    n = pl.program_id(1)
    tc = xa_ref.shape[0]
    pad = SUBLANES

    @pl.when(n == 0)
    def _():
        xext[0:pad, :] = jnp.zeros((pad, LRU_WIDTH), F32)
        hc[...] = jnp.zeros_like(hc)

    xext[pad:pad + tc, :] = xa_ref[...].astype(F32)
    xc = cb_ref[...] + cw_ref[CONV_WIDTH - 1:CONV_WIDTH, :] * xext[pad:pad + tc, :]
    for k in range(CONV_WIDTH - 1):
        off = pad - (CONV_WIDTH - 1) + k
        xc = xc + cw_ref[k:k + 1, :] * xext[off:off + tc, :]
    xext[0:pad, :] = xext[tc:tc + pad, :]

    xb = xc.astype(BF16)
    z = -lam_ref[...]
    sp = jnp.maximum(z, 0.0) + jnp.log1p(jnp.exp(-jnp.abs(z)))
    gw = 2 * MXU_DIM
    for c in range(LRU_WIDTH // MXU_DIM):
        sl = slice(c * MXU_DIM, (c + 1) * MXU_DIM)
        g = jnp.dot(xb[:, sl], wg_ref[c], preferred_element_type=F32)
        r = _sigmoid(g[:, 0:MXU_DIM] + ba_ref[:, sl])
        ig = _sigmoid(g[:, MXU_DIM:gw] + bx_ref[:, sl])
        log_a = (-LRU_C * r) * sp[:, sl]
        a = jnp.exp(log_a)
        a_s[:, sl] = a
        u_s[:, sl] = jnp.sqrt(-jnp.tanh(log_a) * (a * a + 1.0)) * (ig * xc[:, sl])

    def step(t, h):
        h = a_s[pl.ds(t, 1), :] * h + u_s[pl.ds(t, 1), :]
        u_s[pl.ds(t, 1), :] = h
        return h

    hc[...] = lax.fori_loop(0, tc, step, hc[...], unroll=8)
    o_ref[...] = (u_s[...] * ga_ref[...].astype(F32)).astype(BF16)


def _lru(proj, cw, cb, wg, ba, bx, lam, B, S, tc=512):
    T = proj.shape[0]
    nt = S // tc
    W = LRU_WIDTH
    vec = pl.BlockSpec((1, W), lambda b, n: (0, 0))
    est = 2 * 2 * tc * W * 2 + 2 * tc * W * 2 + 3 * tc * W * 4 + 8 * tc * W * 4
    return pl.pallas_call(
        _lru_kernel,
        grid=(B, nt),
        in_specs=[
            pl.BlockSpec((tc, W), lambda b, n: (b * nt + n, SEG_XA)),
            pl.BlockSpec((tc, W), lambda b, n: (b * nt + n, SEG_GA)),
            pl.BlockSpec((CONV_WIDTH, W), lambda b, n: (0, 0)),
            vec,
            pl.BlockSpec((W // MXU_DIM, MXU_DIM, 2 * MXU_DIM), lambda b, n: (0, 0, 0)),
            vec, vec, vec,
        ],
        out_specs=pl.BlockSpec((tc, W), lambda b, n: (b * nt + n, 0)),
        out_shape=jax.ShapeDtypeStruct((T, W), BF16),
        scratch_shapes=[
            pltpu.VMEM((tc + SUBLANES, W), F32),
            pltpu.VMEM((tc, W), F32),
            pltpu.VMEM((tc, W), F32),
            pltpu.VMEM((1, W), F32),
        ],
        compiler_params=pltpu.CompilerParams(
            dimension_semantics=("parallel", "arbitrary"),
            vmem_limit_bytes=_vmem_limit(est)),
        name="lru",
    )(proj, proj, cw, cb, wg, ba, bx, lam)


def _ret_kernel(q_ref, k_ref, v_ref, g_ref, intra_ref, kdec_ref, qdec_ref, o_ref, st_ref):
    n = pl.program_id(1)
    tc = q_ref.shape[0]
    C = RET_CHUNK
    dh = RET_HEAD_DIM

    @pl.when(n == 0)
    def _():
        st_ref[...] = jnp.zeros_like(st_ref)

    nt_dims = (((1,), (1,)), ((), ()))
    for h in range(RET_HEADS):
        sl = slice(h * dh, (h + 1) * dh)
        intra = intra_ref[h]
        kdec = kdec_ref[h]
        qdec = qdec_ref[h]
        cdec = qdec[C - 1:C, :]
        for c in range(tc // C):
            rows = slice(c * C, (c + 1) * C)
            q = q_ref[rows, sl]
            k = k_ref[rows, sl]
            v = v_ref[rows, sl]
            state = st_ref[h]
            scores = lax.dot_general(q, k, nt_dims, preferred_element_type=F32) * intra
            inner = jnp.dot(scores.astype(BF16), v, preferred_element_type=F32)
            qd = (q.astype(F32) * qdec).astype(BF16)
            cross = jnp.dot(qd, state.astype(BF16), preferred_element_type=F32)
            kd_t = (k.astype(F32) * kdec).T.astype(BF16)
            kv = jnp.dot(kd_t, v, preferred_element_type=F32)
            st_ref[h] = state * cdec + kv
            out = inner + cross
            mu = jnp.mean(out, axis=-1, keepdims=True)
            d = out - mu
            var = jnp.mean(d * d, axis=-1, keepdims=True)
            y = d * lax.rsqrt(var + EPS)
            o_ref[rows, sl] = (y * g_ref[rows, sl].astype(F32)).astype(BF16)


def _retention(proj, intra, kdec, qdec, B, S, tc=256):
    T = proj.shape[0]
    nt = S // tc
    W = RET_WIDTH
    C = RET_CHUNK

    def col(seg):
        return pl.BlockSpec((tc, W), lambda b, n: (b * nt + n, seg))

    est = 2 * 5 * tc * W * 2 + 2 * 3 * RET_HEADS * C * C * 4 + RET_HEADS * C * C * 4 + 16 * C * C * 4 * 8
    return pl.pallas_call(
        _ret_kernel,
        grid=(B, nt),
        in_specs=[
            col(SEG_QR), col(SEG_KR), col(SEG_VR), col(SEG_GR),
            pl.BlockSpec((RET_HEADS, C, C), lambda b, n: (0, 0, 0)),
            pl.BlockSpec((RET_HEADS, C, RET_HEAD_DIM), lambda b, n: (0, 0, 0)),
            pl.BlockSpec((RET_HEADS, C, RET_HEAD_DIM), lambda b, n: (0, 0, 0)),
        ],
        out_specs=pl.BlockSpec((tc, W), lambda b, n: (b * nt + n, 0)),
        out_shape=jax.ShapeDtypeStruct((T, W), BF16),
        scratch_shapes=[pltpu.VMEM((RET_HEADS, RET_HEAD_DIM, RET_HEAD_DIM), F32)],
        compiler_params=pltpu.CompilerParams(
            dimension_semantics=("parallel", "arbitrary"),
            vmem_limit_bytes=_vmem_limit(est)),
        name="retention",
    )(proj, proj, proj, proj, intra, kdec, qdec)


def _diff_kernel(q_ref, k_ref, v_ref, g_ref, dl_ref, sub_ref, o_ref, vxt_ref, m_ref, a_ref, s_ref,
                 *, lam_init, cs):
    qi = pl.program_id(2)
    bq = q_ref.shape[0]
    nkb, vx_rows, bk = vxt_ref.shape
    d = DIFF_HEAD_DIM
    dv = 2 * d
    nt_dims = (((1,), (1,)), ((), ()))

    @pl.when(qi == 0)
    def _():
        for c in range(nkb):
            vxt_ref[c, 0:dv, :] = v_ref[c * bk:(c + 1) * bk, :].astype(F32).T.astype(BF16)
            vxt_ref[c, dv:vx_rows, :] = jnp.ones((vx_rows - dv, bk), BF16)

    lane = lax.broadcasted_iota(jnp.int32, (bq, dv), 1)
    qs = q_ref[...].astype(F32) * (d ** -0.5 * math.log2(math.e))
    qq = jnp.concatenate([jnp.where(lane < d, qs, 0.0), jnp.where(lane >= d, qs, 0.0)],
                         axis=0).astype(BF16)

    ncol = 2 * bq // cs

    def visible(c, kv0):
        return kv0 is None or (c * cs) % bq + cs - 1 >= kv0

    def scores(t, slot, kv0=None):
        k = k_ref[pl.ds(pl.multiple_of(t * bk, bk), bk), :]
        for c in range(ncol):
            if visible(c, kv0):
                s_ref[slot, :, c * cs:(c + 1) * cs] = lax.dot_general(
                    k, qq[c * cs:(c + 1) * cs], nt_dims, preferred_element_type=F32)

    def softmax_pv(t, slot, kv0):
        vxt = vxt_ref[t]
        for c in range(ncol):
            if not visible(c, kv0):
                continue
            cols = slice(c * cs, (c + 1) * cs)
            s = s_ref[slot, :, cols]
            if kv0 is not None:
                key = lax.broadcasted_iota(jnp.int32, (bk, cs), 0) + kv0
                qry = lax.broadcasted_iota(jnp.int32, (bk, cs), 1) + (c * cs) % bq
                s = jnp.where(key <= qry, s, -jnp.inf)
            m_prev = m_ref[:, cols]
            m_new = jnp.maximum(m_prev, jnp.max(s, axis=0, keepdims=True))
            alpha = jnp.exp2(m_prev - m_new)
            p = jnp.exp2(s - m_new).astype(BF16)
            pv = jnp.dot(vxt, p, preferred_element_type=F32)
            a_ref[:, cols] = alpha * a_ref[:, cols] + pv
            m_ref[:, cols] = m_new

    ndiag = bq // bk
    nfull = qi * ndiag

    def body(u, carry):
        t = 2 * u
        scores(t + 1, 1)
        softmax_pv(t, 0, None)
        scores(t + 2, 0)
        softmax_pv(t + 1, 1, None)
        return carry

    scores(0, 0)
    m_ref[...] = jnp.full(m_ref.shape, -1e30, F32)
    a_ref[...] = jnp.zeros(a_ref.shape, F32)
    lax.fori_loop(0, nfull // 2, body, 0)
    for jj in range(ndiag):
        if jj + 1 < ndiag:
            scores(nfull + jj + 1, (jj + 1) % 2, (jj + 1) * bk)
        softmax_pv(nfull + jj, jj % 2, jj * bk)

    dl = dl_ref[...].astype(F32)
    lam = (jnp.exp(jnp.sum(dl[0:1, :] * dl[1:2, :], axis=-1, keepdims=True))
           - jnp.exp(jnp.sum(dl[2:3, :] * dl[3:4, :], axis=-1, keepdims=True)) + lam_init)
    o1 = a_ref[0:dv, 0:bq] / a_ref[dv:dv + 1, 0:bq]
    o2 = a_ref[0:dv, bq:2 * bq] / a_ref[dv:dv + 1, bq:2 * bq]
    of = o1 - lam * o2
    of = of * lax.rsqrt(jnp.mean(of * of, axis=0, keepdims=True) + EPS) * sub_ref[...]
    o = (of * (1.0 - lam_init)).T
    o_ref[...] = (o * g_ref[...].astype(F32)).astype(BF16)


def _diff_attn(proj, dl, sub, lam_init, B, S, bq=2048, bk=512, cs=256):
    T = proj.shape[0]
    nq = S // bq
    H = DIFF_HEADS
    dv = DIFF_V_DIM
    per_seg = SEG // dv
    ones_rows = 2 * SUBLANES
    assert bq % (2 * bk) == 0 and S % bq == 0 and bq % cs == 0

    def qcol(seg):
        return pl.BlockSpec((bq, dv), lambda b, h, i: (b * nq + i, seg * per_seg + h))

    def kvcol(seg):
        return pl.BlockSpec((S, dv), lambda b, h, i: (b, seg * per_seg + h))

    est = (2 * 2 * S * dv * 2 + S * (dv + ones_rows) * 2 + 2 * 3 * bq * dv * 2
           + (dv + ones_rows + 1) * 2 * bq * 4 + 2 * bk * 2 * bq * 4 + 8 * bk * cs * 4
           + 6 * bq * dv * 4)
    return pl.pallas_call(
        functools.partial(_diff_kernel, lam_init=lam_init, cs=cs),
        grid=(B, H, nq),
        in_specs=[
            qcol(SEG_QD), kvcol(SEG_KD), kvcol(SEG_VD), qcol(SEG_GD),
            pl.BlockSpec((4, DIFF_HEAD_DIM), lambda b, h, i: (0, 0)),
            pl.BlockSpec((dv, 1), lambda b, h, i: (0, 0)),
        ],
        out_specs=pl.BlockSpec((bq, dv), lambda b, h, i: (b * nq + i, h)),
        out_shape=jax.ShapeDtypeStruct((T, DIFF_WIDTH), BF16),
        scratch_shapes=[
            pltpu.VMEM((S // bk, dv + ones_rows, bk), BF16),
            pltpu.VMEM((1, 2 * bq), F32),
            pltpu.VMEM((dv + ones_rows, 2 * bq), F32),
            pltpu.VMEM((2, bk, 2 * bq), F32),
        ],
        compiler_params=pltpu.CompilerParams(
            dimension_semantics=("parallel", "parallel", "arbitrary"),
            vmem_limit_bytes=_vmem_limit(est)),
        name="diff_attn",
    )(proj, proj, proj, proj, dl, sub)


def _merge_kernel(ya_ref, yb_ref, yc_ref, wa_ref, wb_ref, wc_ref, g0_ref, g1_ref, g2_ref, o_ref):
    m = g0_ref[...].astype(F32) * jnp.dot(ya_ref[...], wa_ref[...], preferred_element_type=F32)
    m = m + g1_ref[...].astype(F32) * jnp.dot(yb_ref[...], wb_ref[...], preferred_element_type=F32)
    m = m + g2_ref[...].astype(F32) * jnp.dot(yc_ref[...], wc_ref[...], preferred_element_type=F32)
    o_ref[...] = m.astype(BF16)


def _merge(ya, yb, yc, wa, wb, wc, proj, bm=512, bn=1024):
    T, W = ya.shape
    D = wa.shape[1]
    ybk = pl.BlockSpec((bm, W), lambda i, j: (i, 0))
    wbk = pl.BlockSpec((W, bn), lambda i, j: (0, j))
    gm0 = SEG_GM * SEG // bn

    def gate(br):
        return pl.BlockSpec((bm, bn), lambda i, j: (i, gm0 + br * (D // bn) + j))

    est = 2 * 3 * bm * W * 2 + 2 * 3 * W * bn * 2 + 2 * 4 * bm * bn * 2 + 6 * bm * bn * 4
    return pl.pallas_call(
        _merge_kernel,
        grid=(T // bm, D // bn),
        in_specs=[ybk, ybk, ybk, wbk, wbk, wbk, gate(0), gate(1), gate(2)],
        out_specs=pl.BlockSpec((bm, bn), lambda i, j: (i, j)),
        out_shape=jax.ShapeDtypeStruct((T, D), BF16),
        compiler_params=pltpu.CompilerParams(
            dimension_semantics=("parallel", "arbitrary"),
            vmem_limit_bytes=_vmem_limit(est)),
        name="merge",
    )(ya, yb, yc, wa, wb, wc, proj, proj, proj)


def _out_kernel(m_ref, w_ref, x_ref, g_ref, o_ref):
    o = jnp.dot(m_ref[...], w_ref[...], preferred_element_type=F32)
    y = o * lax.rsqrt(jnp.mean(o * o, axis=-1, keepdims=True) + EPS)
    o_ref[...] = x_ref[...] + y * g_ref[...]


def _out_proj(m, w, x2, g, bm=256):
    T, D = x2.shape
    est = 2 * bm * D * 2 + 2 * D * D * 2 + 4 * bm * D * 4 + 4 * bm * D * 4
    return pl.pallas_call(
        _out_kernel,
        grid=(T // bm,),
        in_specs=[
            pl.BlockSpec((bm, D), lambda i: (i, 0)),
            pl.BlockSpec((D, D), lambda i: (0, 0)),
            pl.BlockSpec((bm, D), lambda i: (i, 0)),
            pl.BlockSpec((1, D), lambda i: (0, 0)),
        ],
        out_specs=pl.BlockSpec((bm, D), lambda i: (i, 0)),
        out_shape=jax.ShapeDtypeStruct((T, D), F32),
        compiler_params=pltpu.CompilerParams(
            dimension_semantics=("parallel",),
            vmem_limit_bytes=_vmem_limit(est)),
        name="out_proj",
    )(m, w, x2, g)


def _rope_tables(S):
    pos = jnp.arange(S, dtype=F32)
    ret_freq = 1.0 / (RET_ROT_BASE ** jnp.linspace(0.0, 1.0, RET_HEAD_DIM // 2, dtype=F32))
    ang = pos[:, None] * ret_freq[None, :]
    c, s = jnp.cos(ang), jnp.sin(ang)
    zero = jnp.zeros_like(s)
    rc = jnp.stack([c, c], axis=-1).reshape(S, RET_HEAD_DIM)
    rm = jnp.stack([-s, zero], axis=-1).reshape(S, RET_HEAD_DIM)
    rp = jnp.stack([zero, s], axis=-1).reshape(S, RET_HEAD_DIM)
    inv_freq = ROPE_THETA ** (-jnp.arange(0, ROPE_DIM, 2, dtype=F32) / ROPE_DIM)
    angd = pos[:, None] * inv_freq[None, :]
    cd, sd = jnp.cos(angd), jnp.sin(angd)
    half = ROPE_DIM // 2
    rest = DIFF_HEAD_DIM - ROPE_DIM
    one = jnp.ones((S, rest), F32)
    zero_r = jnp.zeros((S, rest), F32)
    zero_h = jnp.zeros((S, half), F32)
    da = jnp.concatenate([cd, cd, one], axis=-1)
    dm = jnp.concatenate([-sd, zero_h, zero_r], axis=-1)
    dp = jnp.concatenate([zero_h, sd, zero_r], axis=-1)
    rep = LANES // DIFF_HEAD_DIM
    return rc, rm, rp, jnp.tile(da, (1, rep)), jnp.tile(dm, (1, rep)), jnp.tile(dp, (1, rep))


def _decay_tables():
    C = RET_CHUNK
    H = RET_HEADS
    scale = RET_HEAD_DIM ** -0.5
    log_g = jnp.log1p(-jnp.exp2(-5.0 - jnp.arange(H, dtype=F32)))
    idx = jnp.arange(C, dtype=F32)
    rel = idx[:, None] - idx[None, :]
    intra = jnp.where(rel[None] >= 0,
                      jnp.exp(log_g[:, None, None] * jnp.maximum(rel, 0.0)[None]), 0.0) * scale
    k_decay = jnp.exp(log_g[:, None] * (C - 1.0 - idx)[None, :]) * scale
    q_decay = jnp.exp(log_g[:, None] * (idx + 1.0)[None, :])
    kdec = jnp.broadcast_to(k_decay[:, :, None], (H, C, RET_HEAD_DIM))
    qdec = jnp.broadcast_to(q_decay[:, :, None], (H, C, RET_HEAD_DIM))
    return intra, kdec, qdec


def _gate_weights(wa, wx):
    per = MXU_DIM // LRU_BLOCK
    nt = LRU_BLOCKS // per

    def dense(w):
        w4 = w.reshape(nt, per, LRU_BLOCK, LRU_BLOCK)
        eye = jnp.eye(per, dtype=w.dtype)
        d = jnp.einsum('tpde,pq->tpdqe', w4, eye)
        return d.reshape(nt, MXU_DIM, MXU_DIM)

    return jnp.concatenate([dense(wa), dense(wx)], axis=-1).astype(BF16)


def kernel(x, pre_norm, post_norm, w_in, conv_w, conv_b, lru_wa, lru_ba, lru_wx, lru_bx,
           lru_lambda, diff_lambda, diff_subln, w_branch_a, w_branch_b, w_branch_c, w_out):
    B, S, D = x.shape
    depth = w_in.shape[0]
    T = B * S
    tabs = _rope_tables(S)
    intra, kdec, qdec = _decay_tables()
    x2 = x.reshape(T, D)
    for l in range(depth):
        lam_init = 0.8 - 0.6 * math.exp(-0.3 * l)
        proj = _in_proj(x2, pre_norm[l][None, :], w_in[l].astype(BF16), tabs, S)
        ya = _lru(proj, conv_w[l], conv_b[l][None, :], _gate_weights(lru_wa[l], lru_wx[l]),
                  lru_ba[l][None, :], lru_bx[l][None, :], lru_lambda[l][None, :], B, S)
        yb = _retention(proj, intra, kdec, qdec, B, S)
        yc = _diff_attn(proj, diff_lambda[l], diff_subln[l][:, None], lam_init, B, S)
        m = _merge(ya, yb, yc, w_branch_a[l].astype(BF16), w_branch_b[l].astype(BF16),
                   w_branch_c[l].astype(BF16), proj)
        x2 = _out_proj(m, w_out[l].astype(BF16), x2, post_norm[l][None, :])
    return x2.reshape(B, S, D)
```

```python
import functools
import math

import numpy as np
import jax
import jax.numpy as jnp
from jax import lax
from jax.experimental import pallas as pl
from jax.experimental.pallas import tpu as pltpu

F32 = jnp.float32
BF16 = jnp.bfloat16

D_MODEL = 2048
LRU_WIDTH = 1024
LRU_BLOCKS = 16
LRU_BLOCK = LRU_WIDTH // LRU_BLOCKS
CONV_WIDTH = 4
LRU_C = 8.0
RET_HEADS = 8
RET_HEAD_DIM = 128
RET_WIDTH = RET_HEADS * RET_HEAD_DIM
RET_CHUNK = 128
RET_ROT_BASE = 10000.0
DIFF_HEADS = 8
DIFF_HEAD_DIM = 64
DIFF_V_DIM = 2 * DIFF_HEAD_DIM
DIFF_WIDTH = DIFF_HEADS * DIFF_V_DIM
ROPE_THETA = 500000.0
ROPE_DIM = DIFF_HEAD_DIM // 4
N_BRANCH = 3
EPS = 1e-6
D_IN = 2 * LRU_WIDTH + 4 * RET_WIDTH + 4 * DIFF_WIDTH + N_BRANCH * D_MODEL

LANES = 128
SUBLANES = 8
MXU_DIM = 256
VMEM_BYTES_V7X = 64 * 1024 * 1024

SEG = 1024
NORM_ROWS = 256
OUT_ROWS = 256
PROJ_ROWS = 256
SEG_XA, SEG_GA, SEG_QR, SEG_KR, SEG_VR, SEG_GR, SEG_QD, SEG_KD, SEG_VD, SEG_GD, SEG_GM = range(11)


def _sigmoid(x):
    return 1.0 / (1.0 + jnp.exp(-x))


def _vmem_limit(nbytes):
    return int(min(max(nbytes, 16 * 1024 * 1024), VMEM_BYTES_V7X - 8 * 1024 * 1024))


def _in_proj_kernel(x_ref, g_ref, w_ref, rc_ref, rm_ref, rp_ref, da_ref, dm_ref, dp_ref, o_ref, h_ref):
    j = pl.program_id(1)

    @pl.when(j == 0)
    def _():
        for r in range(x_ref.shape[0] // NORM_ROWS):
            rows = slice(r * NORM_ROWS, (r + 1) * NORM_ROWS)
            xf = x_ref[rows, :]
            ms = jnp.mean(xf * xf, axis=-1, keepdims=True)
            h_ref[rows, :] = ((xf * lax.rsqrt(ms + EPS)) * g_ref[...]).astype(BF16)

    def project(epilogue):
        bm = o_ref.shape[0]
        rsub = min(bm, PROJ_ROWS)
        for c in range(o_ref.shape[1] // MXU_DIM):
            cols = slice(c * MXU_DIM, (c + 1) * MXU_DIM)
            for r in range(bm // rsub):
                rows = slice(r * rsub, (r + 1) * rsub)
                acc = jnp.dot(h_ref[rows, :], w_ref[:, cols], preferred_element_type=F32)
                o_ref[rows, cols] = epilogue(acc, rows).astype(BF16)

    def per_lane_chunk(fn):
        def epilogue(acc, rows):
            return jnp.concatenate(
                [fn(acc[:, c * LANES:(c + 1) * LANES], rows) for c in range(MXU_DIM // LANES)],
                axis=1)
        return epilogue

    is_ret = jnp.logical_or(j == SEG_QR, j == SEG_KR)
    is_dif = jnp.logical_or(j == SEG_QD, j == SEG_KD)
    is_silu = jnp.logical_or(jnp.logical_or(j == SEG_GA, j == SEG_GR), j == SEG_GD)
    is_sig = j >= SEG_GM
    is_plain = jnp.logical_or(jnp.logical_or(j == SEG_XA, j == SEG_VR), j == SEG_VD)

    @pl.when(is_plain)
    def _():
        project(lambda acc, rows: acc)

    @pl.when(is_silu)
    def _():
        project(lambda acc, rows: acc * _sigmoid(acc))

    @pl.when(is_sig)
    def _():
        project(lambda acc, rows: _sigmoid(acc))

    @pl.when(is_ret)
    def _():
        project(per_lane_chunk(
            lambda xc, rows: xc * rc_ref[rows, :] + pltpu.roll(xc, LANES - 1, 1) * rm_ref[rows, :]
            + pltpu.roll(xc, 1, 1) * rp_ref[rows, :]))

    @pl.when(is_dif)
    def _():
        half = ROPE_DIM // 2
        project(per_lane_chunk(
            lambda xc, rows: xc * da_ref[rows, :] + pltpu.roll(xc, LANES - half, 1) * dm_ref[rows, :]
            + pltpu.roll(xc, half, 1) * dp_ref[rows, :]))


def _in_proj(x2, g, w, tabs, S, bm=1024, bn=SEG):
    T, D = x2.shape
    N = w.shape[1]
    assert bn == SEG and T % bm == 0 and S % bm == 0 and N % bn == 0
    nb = S // bm
    tab_spec = pl.BlockSpec((bm, LANES), lambda i, j: (i % nb, 0))
    est = (2 * bm * D * 4 + bm * D * 2 + 2 * D * bn * 2 + 2 * bm * bn * 2
           + 12 * bm * LANES * 4 + 8 * bm * MXU_DIM * 4 + 4 * NORM_ROWS * D * 4)
    return pl.pallas_call(
        _in_proj_kernel,
        grid=(T // bm, N // bn),
        in_specs=[
            pl.BlockSpec((bm, D), lambda i, j: (i, 0)),
            pl.BlockSpec((1, D), lambda i, j: (0, 0)),
            pl.BlockSpec((D, bn), lambda i, j: (0, j)),
            tab_spec, tab_spec, tab_spec, tab_spec, tab_spec, tab_spec,
        ],
        out_specs=pl.BlockSpec((bm, bn), lambda i, j: (i, j)),
        out_shape=jax.ShapeDtypeStruct((T, N), BF16),
        scratch_shapes=[pltpu.VMEM((bm, D), BF16)],
        compiler_params=pltpu.CompilerParams(
            dimension_semantics=("parallel", "arbitrary"),
            vmem_limit_bytes=_vmem_limit(est)),
        name="in_proj",
    )(x2, g, w, *tabs)


def _lru_kernel(xa_ref, ga_ref, cw_ref, cb_ref, wg_ref, ba_ref, bx_ref, lam_ref, o_ref,
                xext, a_s, u_s, hc):
    n = pl.program_id(1)
    nb, tc, _ = xa_ref.shape
    pad = SUBLANES

    @pl.when(n == 0)
    def _():
        xext[:, 0:pad, :] = jnp.zeros((nb, pad, LRU_WIDTH), F32)
        hc[...] = jnp.zeros_like(hc)

    z = -lam_ref[...]
    sp = jnp.maximum(z, 0.0) + jnp.log1p(jnp.exp(-jnp.abs(z)))
    gw = 2 * MXU_DIM
    for bi in range(nb):
        xext[bi, pad:pad + tc, :] = xa_ref[bi].astype(F32)
        xc = cb_ref[...] + cw_ref[CONV_WIDTH - 1:CONV_WIDTH, :] * xext[bi, pad:pad + tc, :]
        for k in range(CONV_WIDTH - 1):
            off = pad - (CONV_WIDTH - 1) + k
            xc = xc + cw_ref[k:k + 1, :] * xext[bi, off:off + tc, :]
        xext[bi, 0:pad, :] = xext[bi, tc:tc + pad, :]

        xb = xc.astype(BF16)
        for c in range(LRU_WIDTH // MXU_DIM):
            sl = slice(c * MXU_DIM, (c + 1) * MXU_DIM)
            g = jnp.dot(xb[:, sl], wg_ref[c], preferred_element_type=F32)
            r = _sigmoid(g[:, 0:MXU_DIM] + ba_ref[:, sl])
            ig = _sigmoid(g[:, MXU_DIM:gw] + bx_ref[:, sl])
            log_a = (-LRU_C * r) * sp[:, sl]
            a = jnp.exp(log_a)
            a_s[bi, :, sl] = a
            u_s[bi, :, sl] = jnp.sqrt(-jnp.tanh(log_a) * (a * a + 1.0)) * (ig * xc[:, sl])

    def step(t, hs):
        out = []
        for bi in range(nb):
            h = a_s[bi, pl.ds(t, 1), :] * hs[bi] + u_s[bi, pl.ds(t, 1), :]
            u_s[bi, pl.ds(t, 1), :] = h
            out.append(h)
        return tuple(out)

    hs = lax.fori_loop(0, tc, step, tuple(hc[bi] for bi in range(nb)), unroll=8)
    for bi in range(nb):
        hc[bi] = hs[bi]
        o_ref[bi] = (u_s[bi] * ga_ref[bi].astype(F32)).astype(BF16)


def _lru(proj, cw, cb, wg, ba, bx, lam, B, S, tc=256, nb=4):
    T = proj.shape[0]
    W = LRU_WIDTH
    assert B % nb == 0 and S % tc == 0
    proj3 = proj.reshape(B, S, proj.shape[1])
    vec = pl.BlockSpec((1, W), lambda b, n: (0, 0))
    est = 2 * 2 * nb * tc * W * 2 + 2 * nb * tc * W * 2 + 3 * nb * tc * W * 4 + 8 * tc * W * 4
    out = pl.pallas_call(
        _lru_kernel,
        grid=(B // nb, S // tc),
        in_specs=[
            pl.BlockSpec((nb, tc, W), lambda b, n: (b, n, SEG_XA)),
            pl.BlockSpec((nb, tc, W), lambda b, n: (b, n, SEG_GA)),
            pl.BlockSpec((CONV_WIDTH, W), lambda b, n: (0, 0)),
            vec,
            pl.BlockSpec((W // MXU_DIM, MXU_DIM, 2 * MXU_DIM), lambda b, n: (0, 0, 0)),
            vec, vec, vec,
        ],
        out_specs=pl.BlockSpec((nb, tc, W), lambda b, n: (b, n, 0)),
        out_shape=jax.ShapeDtypeStruct((B, S, W), BF16),
        scratch_shapes=[
            pltpu.VMEM((nb, tc + SUBLANES, W), F32),
            pltpu.VMEM((nb, tc, W), F32),
            pltpu.VMEM((nb, tc, W), F32),
            pltpu.VMEM((nb, 1, W), F32),
        ],
        compiler_params=pltpu.CompilerParams(
            dimension_semantics=("parallel", "arbitrary"),
            vmem_limit_bytes=_vmem_limit(est)),
        name="lru",
    )(proj3, proj3, cw, cb, wg, ba, bx, lam)
    return out.reshape(T, W)


def _ret_kernel(q_ref, k_ref, v_ref, g_ref, intra_ref, kdec_ref, qdec_ref, o_ref, st_ref):
    n = pl.program_id(1)
    tc = q_ref.shape[0]
    C = RET_CHUNK
    dh = RET_HEAD_DIM

    @pl.when(n == 0)
    def _():
        st_ref[...] = jnp.zeros_like(st_ref)

    nt_dims = (((1,), (1,)), ((), ()))
    for h in range(RET_HEADS):
        sl = slice(h * dh, (h + 1) * dh)
        intra = intra_ref[h]
        kdec = kdec_ref[h]
        qdec = qdec_ref[h]
        cdec = qdec[C - 1:C, :]
        for c in range(tc // C):
            rows = slice(c * C, (c + 1) * C)
            q = q_ref[rows, sl]
            k = k_ref[rows, sl]
            v = v_ref[rows, sl]
            state = st_ref[h]
            scores = lax.dot_general(q, k, nt_dims, preferred_element_type=F32) * intra
            inner = jnp.dot(scores.astype(BF16), v, preferred_element_type=F32)
            qd = (q.astype(F32) * qdec).astype(BF16)
            cross = jnp.dot(qd, state.astype(BF16), preferred_element_type=F32)
            kd_t = (k.astype(F32) * kdec).T.astype(BF16)
            kv = jnp.dot(kd_t, v, preferred_element_type=F32)
            st_ref[h] = state * cdec + kv
            out = inner + cross
            mu = jnp.mean(out, axis=-1, keepdims=True)
            d = out - mu
            var = jnp.mean(d * d, axis=-1, keepdims=True)
            y = d * lax.rsqrt(var + EPS)
            o_ref[rows, sl] = (y * g_ref[rows, sl].astype(F32)).astype(BF16)


def _retention(proj, intra, kdec, qdec, B, S, tc=512):
    T = proj.shape[0]
    nt = S // tc
    W = RET_WIDTH
    C = RET_CHUNK

    def col(seg):
        return pl.BlockSpec((tc, W), lambda b, n: (b * nt + n, seg))

    est = 2 * 5 * tc * W * 2 + 2 * 3 * RET_HEADS * C * C * 4 + RET_HEADS * C * C * 4 + 16 * C * C * 4 * 8
    return pl.pallas_call(
        _ret_kernel,
        grid=(B, nt),
        in_specs=[
            col(SEG_QR), col(SEG_KR), col(SEG_VR), col(SEG_GR),
            pl.BlockSpec((RET_HEADS, C, C), lambda b, n: (0, 0, 0)),
            pl.BlockSpec((RET_HEADS, C, RET_HEAD_DIM), lambda b, n: (0, 0, 0)),
            pl.BlockSpec((RET_HEADS, C, RET_HEAD_DIM), lambda b, n: (0, 0, 0)),
        ],
        out_specs=pl.BlockSpec((tc, W), lambda b, n: (b * nt + n, 0)),
        out_shape=jax.ShapeDtypeStruct((T, W), BF16),
        scratch_shapes=[pltpu.VMEM((RET_HEADS, RET_HEAD_DIM, RET_HEAD_DIM), F32)],
        compiler_params=pltpu.CompilerParams(
            dimension_semantics=("parallel", "arbitrary"),
            vmem_limit_bytes=_vmem_limit(est)),
        name="retention",
    )(proj, proj, proj, proj, intra, kdec, qdec)


def _diff_kernel(q_ref, k_ref, v_ref, g_ref, dl_ref, sub_ref, o_ref, vxt_ref, m_ref, a_ref, s_ref,
                 tri_ref, qq_ref, *, lam_init, cs):
    qi = pl.program_id(2)
    bq = q_ref.shape[0]
    nkb, vx_rows, bk = vxt_ref.shape
    d = DIFF_HEAD_DIM
    dv = 2 * d
    nt_dims = (((1,), (1,)), ((), ()))

    @pl.when(qi == 0)
    def _():
        for c in range(nkb):
            vxt_ref[c, 0:dv, :] = v_ref[c * bk:(c + 1) * bk, :].astype(F32).T.astype(BF16)
            vxt_ref[c, dv:vx_rows, :] = jnp.ones((vx_rows - dv, bk), BF16)

    def prep_queries(c):
        lane = lax.broadcasted_iota(jnp.int32, (cs, dv), 1)
        q0 = (c * cs) % bq
        qs = q_ref[q0:q0 + cs, :].astype(F32) * (d ** -0.5 * math.log2(math.e))
        keep = (lane < d) if c * cs < bq else (lane >= d)
        qq_ref[c * cs:(c + 1) * cs, :] = jnp.where(keep, qs, 0.0).astype(BF16)

    ncol = 2 * bq // cs

    tri_ref[...] = jnp.where(lax.broadcasted_iota(jnp.int32, (cs, cs), 0)
                             <= lax.broadcasted_iota(jnp.int32, (cs, cs), 1), 0.0, -jnp.inf)

    def key_plan(c, kv0):
        nsubk = bk // cs
        if kv0 is None:
            return nsubk, False
        delta = (c * cs) % bq - kv0
        if delta < 0:
            return 0, False
        nfullk = min(nsubk, delta // cs)
        return nfullk, nfullk < nsubk

    def scores(t, slot, kv0=None, first=False):
        k = k_ref[pl.ds(pl.multiple_of(t * bk, bk), bk), :]
        for c in range(ncol):
            if first:
                prep_queries(c)
            nfullk, tri = key_plan(c, kv0)
            used = (nfullk + tri) * cs
            if used:
                s_ref[slot, 0:used, c * cs:(c + 1) * cs] = lax.dot_general(
                    k[0:used], qq_ref[c * cs:(c + 1) * cs, :], nt_dims,
                    preferred_element_type=F32)

    def softmax_pv(t, slot, kv0):
        vxt = vxt_ref[t]
        for c in range(ncol):
            nfullk, tri = key_plan(c, kv0)
            used = (nfullk + tri) * cs
            if not used:
                continue
            cols = slice(c * cs, (c + 1) * cs)
            parts = []
            if nfullk:
                parts.append(s_ref[slot, 0:nfullk * cs, cols])
            if tri:
                parts.append(s_ref[slot, nfullk * cs:used, cols] + tri_ref[...])
            m_new = m_prev = m_ref[:, cols]
            for s in parts:
                m_new = jnp.maximum(m_new, jnp.max(s, axis=0, keepdims=True))
            alpha = jnp.exp2(m_prev - m_new)
            p = [jnp.exp2(s - m_new).astype(BF16) for s in parts]
            p = p[0] if len(p) == 1 else jnp.concatenate(p, axis=0)
            pv = jnp.dot(vxt[:, 0:used], p, preferred_element_type=F32)
            a_ref[:, cols] = alpha * a_ref[:, cols] + pv
            m_ref[:, cols] = m_new

    ndiag = bq // bk
    nfull = qi * ndiag

    def body(u, carry):
        t = 2 * u
        scores(t + 1, 1)
        softmax_pv(t, 0, None)
        scores(t + 2, 0)
        softmax_pv(t + 1, 1, None)
        return carry

    scores(0, 0, first=True)
    m_ref[...] = jnp.full(m_ref.shape, -1e30, F32)
    a_ref[...] = jnp.zeros(a_ref.shape, F32)
    lax.fori_loop(0, nfull // 2, body, 0)
    for jj in range(ndiag):
        if jj + 1 < ndiag:
            scores(nfull + jj + 1, (jj + 1) % 2, (jj + 1) * bk)
        softmax_pv(nfull + jj, jj % 2, jj * bk)

    dl = dl_ref[...].astype(F32)
    lam = (jnp.exp(jnp.sum(dl[0:1, :] * dl[1:2, :], axis=-1, keepdims=True))
           - jnp.exp(jnp.sum(dl[2:3, :] * dl[3:4, :], axis=-1, keepdims=True)) + lam_init)
    o1 = a_ref[0:dv, 0:bq] / a_ref[dv:dv + 1, 0:bq]
    o2 = a_ref[0:dv, bq:2 * bq] / a_ref[dv:dv + 1, bq:2 * bq]
    of = o1 - lam * o2
    of = of * lax.rsqrt(jnp.mean(of * of, axis=0, keepdims=True) + EPS) * sub_ref[...]
    o = (of * (1.0 - lam_init)).T
    o_ref[...] = (o * g_ref[...].astype(F32)).astype(BF16)


def _diff_attn(proj, dl, sub, lam_init, B, S, bq=2048, bk=512, cs=256):
    T = proj.shape[0]
    nq = S // bq
    H = DIFF_HEADS
    dv = DIFF_V_DIM
    per_seg = SEG // dv
    ones_rows = 2 * SUBLANES
    assert bq % (2 * bk) == 0 and S % bq == 0 and bk % cs == 0

    def qcol(seg):
        return pl.BlockSpec((bq, dv), lambda b, h, i: (b * nq + i, seg * per_seg + h))

    def kvcol(seg):
        return pl.BlockSpec((S, dv), lambda b, h, i: (b, seg * per_seg + h))

    est = (2 * 2 * S * dv * 2 + S * (dv + ones_rows) * 2 + 2 * 3 * bq * dv * 2
           + (dv + ones_rows + 1) * 2 * bq * 4 + 2 * bk * 2 * bq * 4 + cs * cs * 4 + 16 * bk * cs * 4
           + 6 * bq * dv * 4)
    return pl.pallas_call(
        functools.partial(_diff_kernel, lam_init=lam_init, cs=cs),
        grid=(B, H, nq),
        in_specs=[
            qcol(SEG_QD), kvcol(SEG_KD), kvcol(SEG_VD), qcol(SEG_GD),
            pl.BlockSpec((4, DIFF_HEAD_DIM), lambda b, h, i: (0, 0)),
            pl.BlockSpec((dv, 1), lambda b, h, i: (0, 0)),
        ],
        out_specs=pl.BlockSpec((bq, dv), lambda b, h, i: (b * nq + i, h)),
        out_shape=jax.ShapeDtypeStruct((T, DIFF_WIDTH), BF16),
        scratch_shapes=[
            pltpu.VMEM((S // bk, dv + ones_rows, bk), BF16),
            pltpu.VMEM((1, 2 * bq), F32),
            pltpu.VMEM((dv + ones_rows, 2 * bq), F32),
            pltpu.VMEM((2, bk, 2 * bq), F32),
            pltpu.VMEM((cs, cs), F32),
            pltpu.VMEM((2 * bq, dv), BF16),
        ],
        compiler_params=pltpu.CompilerParams(
            dimension_semantics=("parallel", "parallel", "arbitrary"),
            vmem_limit_bytes=_vmem_limit(est)),
        name="diff_attn",
    )(proj, proj, proj, proj, dl, sub)


def _merge_kernel(ya_ref, yb_ref, yc_ref, wa_ref, wb_ref, wc_ref, g0_ref, g1_ref, g2_ref, o_ref):
    for c in range(o_ref.shape[1] // MXU_DIM):
        cols = slice(c * MXU_DIM, (c + 1) * MXU_DIM)
        m = g0_ref[:, cols].astype(F32) * jnp.dot(ya_ref[...], wa_ref[:, cols],
                                                  preferred_element_type=F32)
        m = m + g1_ref[:, cols].astype(F32) * jnp.dot(yb_ref[...], wb_ref[:, cols],
                                                      preferred_element_type=F32)
        m = m + g2_ref[:, cols].astype(F32) * jnp.dot(yc_ref[...], wc_ref[:, cols],
                                                      preferred_element_type=F32)
        o_ref[:, cols] = m.astype(BF16)


def _merge(ya, yb, yc, wa, wb, wc, proj, bm=512):
    T, W = ya.shape
    D = wa.shape[1]
    ybk = pl.BlockSpec((bm, W), lambda i: (i, 0))
    wbk = pl.BlockSpec((W, D), lambda i: (0, 0), pipeline_mode=pl.Buffered(1))
    gm0 = SEG_GM * SEG // D

    def gate(br):
        return pl.BlockSpec((bm, D), lambda i: (i, gm0 + br))

    est = 2 * 3 * bm * W * 2 + 3 * W * D * 2 + 2 * 4 * bm * D * 2 + 8 * bm * MXU_DIM * 4
    return pl.pallas_call(
        _merge_kernel,
        grid=(T // bm,),
        in_specs=[ybk, ybk, ybk, wbk, wbk, wbk, gate(0), gate(1), gate(2)],
        out_specs=pl.BlockSpec((bm, D), lambda i: (i, 0)),
        out_shape=jax.ShapeDtypeStruct((T, D), BF16),
        compiler_params=pltpu.CompilerParams(
            dimension_semantics=("parallel",),
            vmem_limit_bytes=_vmem_limit(est)),
        name="merge",
    )(ya, yb, yc, wa, wb, wc, proj, proj, proj)


def _out_kernel(m_ref, w_ref, x_ref, g_ref, o_ref):
    for r in range(o_ref.shape[0] // OUT_ROWS):
        rows = slice(r * OUT_ROWS, (r + 1) * OUT_ROWS)
        o = jnp.dot(m_ref[rows, :], w_ref[...], preferred_element_type=F32)
        y = o * lax.rsqrt(jnp.mean(o * o, axis=-1, keepdims=True) + EPS)
        o_ref[rows, :] = x_ref[rows, :] + y * g_ref[...]


def _out_proj(m, w, x2, g, bm=512):
    T, D = x2.shape
    est = 2 * bm * D * 2 + D * D * 2 + 4 * bm * D * 4 + 6 * OUT_ROWS * D * 4
    return pl.pallas_call(
        _out_kernel,
        grid=(T // bm,),
        in_specs=[
            pl.BlockSpec((bm, D), lambda i: (i, 0)),
            pl.BlockSpec((D, D), lambda i: (0, 0), pipeline_mode=pl.Buffered(1)),
            pl.BlockSpec((bm, D), lambda i: (i, 0)),
            pl.BlockSpec((1, D), lambda i: (0, 0)),
        ],
        out_specs=pl.BlockSpec((bm, D), lambda i: (i, 0)),
        out_shape=jax.ShapeDtypeStruct((T, D), F32),
        compiler_params=pltpu.CompilerParams(
            dimension_semantics=("parallel",),
            vmem_limit_bytes=_vmem_limit(est)),
        name="out_proj",
    )(m, w, x2, g)


def _rope_tables(S):
    pos = jnp.arange(S, dtype=F32)
    ret_freq = 1.0 / (RET_ROT_BASE ** jnp.linspace(0.0, 1.0, RET_HEAD_DIM // 2, dtype=F32))
    ang = pos[:, None] * ret_freq[None, :]
    c, s = jnp.cos(ang), jnp.sin(ang)
    zero = jnp.zeros_like(s)
    rc = jnp.stack([c, c], axis=-1).reshape(S, RET_HEAD_DIM)
    rm = jnp.stack([-s, zero], axis=-1).reshape(S, RET_HEAD_DIM)
    rp = jnp.stack([zero, s], axis=-1).reshape(S, RET_HEAD_DIM)
    inv_freq = ROPE_THETA ** (-jnp.arange(0, ROPE_DIM, 2, dtype=F32) / ROPE_DIM)
    angd = pos[:, None] * inv_freq[None, :]
    cd, sd = jnp.cos(angd), jnp.sin(angd)
    half = ROPE_DIM // 2
    rest = DIFF_HEAD_DIM - ROPE_DIM
    one = jnp.ones((S, rest), F32)
    zero_r = jnp.zeros((S, rest), F32)
    zero_h = jnp.zeros((S, half), F32)
    da = jnp.concatenate([cd, cd, one], axis=-1)
    dm = jnp.concatenate([-sd, zero_h, zero_r], axis=-1)
    dp = jnp.concatenate([zero_h, sd, zero_r], axis=-1)
    rep = LANES // DIFF_HEAD_DIM
    return rc, rm, rp, jnp.tile(da, (1, rep)), jnp.tile(dm, (1, rep)), jnp.tile(dp, (1, rep))


def _decay_tables():
    C = RET_CHUNK
    H = RET_HEADS
    scale = RET_HEAD_DIM ** -0.5
    log_g = jnp.log1p(-jnp.exp2(-5.0 - jnp.arange(H, dtype=F32)))
    idx = jnp.arange(C, dtype=F32)
    rel = idx[:, None] - idx[None, :]
    intra = jnp.where(rel[None] >= 0,
                      jnp.exp(log_g[:, None, None] * jnp.maximum(rel, 0.0)[None]), 0.0) * scale
    k_decay = jnp.exp(log_g[:, None] * (C - 1.0 - idx)[None, :]) * scale
    q_decay = jnp.exp(log_g[:, None] * (idx + 1.0)[None, :])
    kdec = jnp.broadcast_to(k_decay[:, :, None], (H, C, RET_HEAD_DIM))
    qdec = jnp.broadcast_to(q_decay[:, :, None], (H, C, RET_HEAD_DIM))
    return intra, kdec, qdec


def _gate_weights(wa, wx):
    per = MXU_DIM // LRU_BLOCK
    nt = LRU_BLOCKS // per

    def dense(w):
        w4 = w.reshape(nt, per, LRU_BLOCK, LRU_BLOCK)
        eye = jnp.eye(per, dtype=w.dtype)
        d = jnp.einsum('tpde,pq->tpdqe', w4, eye)
        return d.reshape(nt, MXU_DIM, MXU_DIM)

    return jnp.concatenate([dense(wa), dense(wx)], axis=-1).astype(BF16)


def kernel(x, pre_norm, post_norm, w_in, conv_w, conv_b, lru_wa, lru_ba, lru_wx, lru_bx,
           lru_lambda, diff_lambda, diff_subln, w_branch_a, w_branch_b, w_branch_c, w_out):
    B, S, D = x.shape
    depth = w_in.shape[0]
    T = B * S
    tabs = _rope_tables(S)
    intra, kdec, qdec = _decay_tables()
    x2 = x.reshape(T, D)
    for l in range(depth):
        lam_init = 0.8 - 0.6 * math.exp(-0.3 * l)
        proj = _in_proj(x2, pre_norm[l][None, :], w_in[l].astype(BF16), tabs, S)
        ya = _lru(proj, conv_w[l], conv_b[l][None, :], _gate_weights(lru_wa[l], lru_wx[l]),
                  lru_ba[l][None, :], lru_bx[l][None, :], lru_lambda[l][None, :], B, S)
        yb = _retention(proj, intra, kdec, qdec, B, S)
        yc = _diff_attn(proj, diff_lambda[l], diff_subln[l][:, None], lam_init, B, S)
        m = _merge(ya, yb, yc, w_branch_a[l].astype(BF16), w_branch_b[l].astype(BF16),
                   w_branch_c[l].astype(BF16), proj)
        x2 = _out_proj(m, w_out[l].astype(BF16), x2, post_norm[l][None, :])
    return x2.reshape(B, S, D)
```

```python
import functools
import math

import numpy as np
import jax
import jax.numpy as jnp
from jax import lax
from jax.experimental import pallas as pl
from jax.experimental.pallas import tpu as pltpu

F32 = jnp.float32
BF16 = jnp.bfloat16

D_MODEL = 2048
LRU_WIDTH = 1024
LRU_BLOCKS = 16
LRU_BLOCK = LRU_WIDTH // LRU_BLOCKS
CONV_WIDTH = 4
LRU_C = 8.0
RET_HEADS = 8
RET_HEAD_DIM = 128
RET_WIDTH = RET_HEADS * RET_HEAD_DIM
RET_CHUNK = 128
RET_ROT_BASE = 10000.0
DIFF_HEADS = 8
DIFF_HEAD_DIM = 64
DIFF_V_DIM = 2 * DIFF_HEAD_DIM
DIFF_WIDTH = DIFF_HEADS * DIFF_V_DIM
ROPE_THETA = 500000.0
ROPE_DIM = DIFF_HEAD_DIM // 4
N_BRANCH = 3
EPS = 1e-6
D_IN = 2 * LRU_WIDTH + 4 * RET_WIDTH + 4 * DIFF_WIDTH + N_BRANCH * D_MODEL

LANES = 128
SUBLANES = 8
MXU_DIM = 256
VMEM_BYTES_V7X = 64 * 1024 * 1024

SEG = 1024
NORM_ROWS = 256
OUT_ROWS = 256
PROJ_ROWS = 256
SEG_XA, SEG_GA, SEG_QR, SEG_KR, SEG_VR, SEG_GR, SEG_QD, SEG_KD, SEG_VD, SEG_GD, SEG_GM = range(11)
SEG_KINDS = ("plain", "silu", "ret", "ret", "plain", "silu", "dif", "dif", "plain", "silu") + (
    "sigmoid",) * (N_BRANCH * D_MODEL // SEG)


def _sigmoid(x):
    return 1.0 / (1.0 + jnp.exp(-x))


def _vmem_limit(nbytes):
    return int(min(max(nbytes, 16 * 1024 * 1024), VMEM_BYTES_V7X - 8 * 1024 * 1024))


def _in_proj_kernel(x_ref, g_ref, w_ref, rc_ref, rm_ref, rp_ref, da_ref, dm_ref, dp_ref, o_ref, h_ref):
    j = pl.program_id(1)

    @pl.when(j == 0)
    def _():
        for r in range(x_ref.shape[0] // NORM_ROWS):
            rows = slice(r * NORM_ROWS, (r + 1) * NORM_ROWS)
            xf = x_ref[rows, :]
            ms = jnp.mean(xf * xf, axis=-1, keepdims=True)
            h_ref[rows, :] = ((xf * lax.rsqrt(ms + EPS)) * g_ref[...]).astype(BF16)

    def project(epilogues):
        bm, bn = o_ref.shape
        rsub = min(bm, PROJ_ROWS)
        for c in range(bn // MXU_DIM):
            cols = slice(c * MXU_DIM, (c + 1) * MXU_DIM)
            epilogue = epilogues[c * MXU_DIM // SEG]
            for r in range(bm // rsub):
                rows = slice(r * rsub, (r + 1) * rsub)
                acc = jnp.dot(h_ref[rows, :], w_ref[:, cols], preferred_element_type=F32)
                o_ref[rows, cols] = epilogue(acc, rows).astype(BF16)

    def per_lane_chunk(fn):
        def epilogue(acc, rows):
            return jnp.concatenate(
                [fn(acc[:, c * LANES:(c + 1) * LANES], rows) for c in range(MXU_DIM // LANES)],
                axis=1)
        return epilogue

    half = ROPE_DIM // 2
    epilogue_of = {
        "plain": lambda acc, rows: acc,
        "silu": lambda acc, rows: acc * _sigmoid(acc),
        "sigmoid": lambda acc, rows: _sigmoid(acc),
        "ret": per_lane_chunk(
            lambda xc, rows: xc * rc_ref[rows, :] + pltpu.roll(xc, LANES - 1, 1) * rm_ref[rows, :]
            + pltpu.roll(xc, 1, 1) * rp_ref[rows, :]),
        "dif": per_lane_chunk(
            lambda xc, rows: xc * da_ref[rows, :] + pltpu.roll(xc, LANES - half, 1) * dm_ref[rows, :]
            + pltpu.roll(xc, half, 1) * dp_ref[rows, :]),
    }
    nseg = o_ref.shape[1] // SEG
    blocks = {}
    for jv in range(len(SEG_KINDS) // nseg):
        blocks.setdefault(SEG_KINDS[jv * nseg:(jv + 1) * nseg], []).append(jv)
    for kinds, jvs in blocks.items():
        cond = functools.reduce(jnp.logical_or, [j == jv for jv in jvs])

        @pl.when(cond)
        def _(kinds=kinds):
            project([epilogue_of[kind] for kind in kinds])


def _in_proj(x2, g, w, tabs, S, bm=1024, bn=2 * SEG):
    T, D = x2.shape
    N = w.shape[1]
    assert bn % SEG == 0 and T % bm == 0 and S % bm == 0 and N % bn == 0
    nb = S // bm
    tab_spec = pl.BlockSpec((bm, LANES), lambda i, j: (i % nb, 0))
    est = (2 * bm * D * 4 + bm * D * 2 + 2 * D * bn * 2 + 2 * bm * bn * 2
           + 12 * bm * LANES * 4 + 8 * bm * MXU_DIM * 4 + 4 * NORM_ROWS * D * 4)
    return pl.pallas_call(
        _in_proj_kernel,
        grid=(T // bm, N // bn),
        in_specs=[
            pl.BlockSpec((bm, D), lambda i, j: (i, 0)),
            pl.BlockSpec((1, D), lambda i, j: (0, 0)),
            pl.BlockSpec((D, bn), lambda i, j: (0, j)),
            tab_spec, tab_spec, tab_spec, tab_spec, tab_spec, tab_spec,
        ],
        out_specs=pl.BlockSpec((bm, bn), lambda i, j: (i, j)),
        out_shape=jax.ShapeDtypeStruct((T, N), BF16),
        scratch_shapes=[pltpu.VMEM((bm, D), BF16)],
        compiler_params=pltpu.CompilerParams(
            dimension_semantics=("parallel", "arbitrary"),
            vmem_limit_bytes=_vmem_limit(est)),
        name="in_proj",
    )(x2, g, w, *tabs)


def _lru_kernel(xa_ref, ga_ref, cw_ref, cb_ref, wg_ref, ba_ref, bx_ref, lam_ref, o_ref,
                xext, a_s, u_s, hc):
    n = pl.program_id(1)
    nb, tc, _ = xa_ref.shape
    pad = SUBLANES

    @pl.when(n == 0)
    def _():
        xext[:, 0:pad, :] = jnp.zeros((nb, pad, LRU_WIDTH), F32)
        hc[...] = jnp.zeros_like(hc)

    z = -lam_ref[...]
    sp = jnp.maximum(z, 0.0) + jnp.log1p(jnp.exp(-jnp.abs(z)))
    gw = 2 * MXU_DIM
    for bi in range(nb):
        xext[bi, pad:pad + tc, :] = xa_ref[bi].astype(F32)
        xc = cb_ref[...] + cw_ref[CONV_WIDTH - 1:CONV_WIDTH, :] * xext[bi, pad:pad + tc, :]
        for k in range(CONV_WIDTH - 1):
            off = pad - (CONV_WIDTH - 1) + k
            xc = xc + cw_ref[k:k + 1, :] * xext[bi, off:off + tc, :]
        xext[bi, 0:pad, :] = xext[bi, tc:tc + pad, :]

        xb = xc.astype(BF16)
        for c in range(LRU_WIDTH // MXU_DIM):
            sl = slice(c * MXU_DIM, (c + 1) * MXU_DIM)
            g = jnp.dot(xb[:, sl], wg_ref[c], preferred_element_type=F32)
            r = _sigmoid(g[:, 0:MXU_DIM] + ba_ref[:, sl])
            ig = _sigmoid(g[:, MXU_DIM:gw] + bx_ref[:, sl])
            log_a = (-LRU_C * r) * sp[:, sl]
            a = jnp.exp(log_a)
            a_s[bi, :, sl] = a
            u_s[bi, :, sl] = jnp.sqrt(-jnp.tanh(log_a) * (a * a + 1.0)) * (ig * xc[:, sl])

    def step(t, hs):
        out = []
        for bi in range(nb):
            h = a_s[bi, pl.ds(t, 1), :] * hs[bi] + u_s[bi, pl.ds(t, 1), :]
            u_s[bi, pl.ds(t, 1), :] = h
            out.append(h)
        return tuple(out)

    hs = lax.fori_loop(0, tc, step, tuple(hc[bi] for bi in range(nb)), unroll=8)
    for bi in range(nb):
        hc[bi] = hs[bi]
        o_ref[bi] = (u_s[bi] * ga_ref[bi].astype(F32)).astype(BF16)


def _lru(proj, cw, cb, wg, ba, bx, lam, B, S, tc=256, nb=4):
    T = proj.shape[0]
    W = LRU_WIDTH
    assert B % nb == 0 and S % tc == 0
    proj3 = proj.reshape(B, S, proj.shape[1])
    vec = pl.BlockSpec((1, W), lambda b, n: (0, 0))
    est = 2 * 2 * nb * tc * W * 2 + 2 * nb * tc * W * 2 + 3 * nb * tc * W * 4 + 8 * tc * W * 4
    out = pl.pallas_call(
        _lru_kernel,
        grid=(B // nb, S // tc),
        in_specs=[
            pl.BlockSpec((nb, tc, W), lambda b, n: (b, n, SEG_XA)),
            pl.BlockSpec((nb, tc, W), lambda b, n: (b, n, SEG_GA)),
            pl.BlockSpec((CONV_WIDTH, W), lambda b, n: (0, 0)),
            vec,
            pl.BlockSpec((W // MXU_DIM, MXU_DIM, 2 * MXU_DIM), lambda b, n: (0, 0, 0)),
            vec, vec, vec,
        ],
        out_specs=pl.BlockSpec((nb, tc, W), lambda b, n: (b, n, 0)),
        out_shape=jax.ShapeDtypeStruct((B, S, W), BF16),
        scratch_shapes=[
            pltpu.VMEM((nb, tc + SUBLANES, W), F32),
            pltpu.VMEM((nb, tc, W), F32),
            pltpu.VMEM((nb, tc, W), F32),
            pltpu.VMEM((nb, 1, W), F32),
        ],
        compiler_params=pltpu.CompilerParams(
            dimension_semantics=("parallel", "arbitrary"),
            vmem_limit_bytes=_vmem_limit(est)),
        name="lru",
    )(proj3, proj3, cw, cb, wg, ba, bx, lam)
    return out.reshape(T, W)


def _ret_kernel(q_ref, k_ref, v_ref, g_ref, intra_ref, kdec_ref, qdec_ref, o_ref, st_ref):
    n = pl.program_id(1)
    tc = q_ref.shape[0]
    C = RET_CHUNK
    dh = RET_HEAD_DIM

    @pl.when(n == 0)
    def _():
        st_ref[...] = jnp.zeros_like(st_ref)

    nt_dims = (((1,), (1,)), ((), ()))
    for h in range(RET_HEADS):
        sl = slice(h * dh, (h + 1) * dh)
        intra = intra_ref[h]
        kdec = kdec_ref[h]
        qdec = qdec_ref[h]
        cdec = qdec[C - 1:C, :]
        for c in range(tc // C):
            rows = slice(c * C, (c + 1) * C)
            q = q_ref[rows, sl]
            k = k_ref[rows, sl]
            v = v_ref[rows, sl]
            state = st_ref[h]
            scores = lax.dot_general(q, k, nt_dims, preferred_element_type=F32) * intra
            inner = jnp.dot(scores.astype(BF16), v, preferred_element_type=F32)
            qd = (q.astype(F32) * qdec).astype(BF16)
            cross = jnp.dot(qd, state.astype(BF16), preferred_element_type=F32)
            kd_t = (k.astype(F32) * kdec).T.astype(BF16)
            kv = jnp.dot(kd_t, v, preferred_element_type=F32)
            st_ref[h] = state * cdec + kv
            out = inner + cross
            mu = jnp.mean(out, axis=-1, keepdims=True)
            d = out - mu
            var = jnp.mean(d * d, axis=-1, keepdims=True)
            y = d * lax.rsqrt(var + EPS)
            o_ref[rows, sl] = (y * g_ref[rows, sl].astype(F32)).astype(BF16)


def _retention(proj, intra, kdec, qdec, B, S, tc=512):
    T = proj.shape[0]
    nt = S // tc
    W = RET_WIDTH
    C = RET_CHUNK

    def col(seg):
        return pl.BlockSpec((tc, W), lambda b, n: (b * nt + n, seg))

    est = 2 * 5 * tc * W * 2 + 2 * 3 * RET_HEADS * C * C * 4 + RET_HEADS * C * C * 4 + 16 * C * C * 4 * 8
    return pl.pallas_call(
        _ret_kernel,
        grid=(B, nt),
        in_specs=[
            col(SEG_QR), col(SEG_KR), col(SEG_VR), col(SEG_GR),
            pl.BlockSpec((RET_HEADS, C, C), lambda b, n: (0, 0, 0)),
            pl.BlockSpec((RET_HEADS, C, RET_HEAD_DIM), lambda b, n: (0, 0, 0)),
            pl.BlockSpec((RET_HEADS, C, RET_HEAD_DIM), lambda b, n: (0, 0, 0)),
        ],
        out_specs=pl.BlockSpec((tc, W), lambda b, n: (b * nt + n, 0)),
        out_shape=jax.ShapeDtypeStruct((T, W), BF16),
        scratch_shapes=[pltpu.VMEM((RET_HEADS, RET_HEAD_DIM, RET_HEAD_DIM), F32)],
        compiler_params=pltpu.CompilerParams(
            dimension_semantics=("parallel", "arbitrary"),
            vmem_limit_bytes=_vmem_limit(est)),
        name="retention",
    )(proj, proj, proj, proj, intra, kdec, qdec)


def _diff_kernel(q_ref, k_ref, v_ref, g_ref, dl_ref, sub_ref, o_ref, vxt_ref, m_ref, a_ref, s_ref,
                 tri_ref, qq_ref, *, lam_init, cs):
    qi = pl.program_id(2)
    bq = q_ref.shape[0]
    nkb, vx_rows, bk = vxt_ref.shape
    d = DIFF_HEAD_DIM
    dv = 2 * d
    nt_dims = (((1,), (1,)), ((), ()))

    @pl.when(qi == 0)
    def _():
        for c in range(nkb):
            vxt_ref[c, 0:dv, :] = v_ref[c * bk:(c + 1) * bk, :].astype(F32).T.astype(BF16)
            vxt_ref[c, dv:vx_rows, :] = jnp.ones((vx_rows - dv, bk), BF16)

    def prep_queries(c):
        lane = lax.broadcasted_iota(jnp.int32, (cs, dv), 1)
        q0 = (c * cs) % bq
        qs = q_ref[q0:q0 + cs, :].astype(F32) * (d ** -0.5 * math.log2(math.e))
        keep = (lane < d) if c * cs < bq else (lane >= d)
        qq_ref[c * cs:(c + 1) * cs, :] = jnp.where(keep, qs, 0.0).astype(BF16)

    ncol = 2 * bq // cs

    tri_ref[...] = jnp.where(lax.broadcasted_iota(jnp.int32, (cs, cs), 0)
                             <= lax.broadcasted_iota(jnp.int32, (cs, cs), 1), 0.0, -jnp.inf)

    def key_plan(c, kv0):
        nsubk = bk // cs
        if kv0 is None:
            return nsubk, False
        delta = (c * cs) % bq - kv0
        if delta < 0:
            return 0, False
        nfullk = min(nsubk, delta // cs)
        return nfullk, nfullk < nsubk

    def scores(t, slot, kv0=None, first=False):
        k = k_ref[pl.ds(pl.multiple_of(t * bk, bk), bk), :]
        for c in range(ncol):
            if first:
                prep_queries(c)
            nfullk, tri = key_plan(c, kv0)
            used = (nfullk + tri) * cs
            if used:
                s_ref[slot, 0:used, c * cs:(c + 1) * cs] = lax.dot_general(
                    k[0:used], qq_ref[c * cs:(c + 1) * cs, :], nt_dims,
                    preferred_element_type=F32)

    def softmax_pv(t, slot, kv0):
        vxt = vxt_ref[t]
        for c in range(ncol):
            nfullk, tri = key_plan(c, kv0)
            used = (nfullk + tri) * cs
            if not used:
                continue
            cols = slice(c * cs, (c + 1) * cs)
            parts = []
            if nfullk:
                parts.append(s_ref[slot, 0:nfullk * cs, cols])
            if tri:
                parts.append(s_ref[slot, nfullk * cs:used, cols] + tri_ref[...])
            m_new = m_prev = m_ref[:, cols]
            for s in parts:
                m_new = jnp.maximum(m_new, jnp.max(s, axis=0, keepdims=True))
            alpha = jnp.exp2(m_prev - m_new)
            p = [jnp.exp2(s - m_new).astype(BF16) for s in parts]
            p = p[0] if len(p) == 1 else jnp.concatenate(p, axis=0)
            pv = jnp.dot(vxt[:, 0:used], p, preferred_element_type=F32)
            a_ref[:, cols] = alpha * a_ref[:, cols] + pv
            m_ref[:, cols] = m_new

    ndiag = bq // bk
    nfull = qi * ndiag

    def body(u, carry):
        t = 2 * u
        scores(t + 1, 1)
        softmax_pv(t, 0, None)
        scores(t + 2, 0)
        softmax_pv(t + 1, 1, None)
        return carry

    scores(0, 0, first=True)
    m_ref[...] = jnp.full(m_ref.shape, -1e30, F32)
    a_ref[...] = jnp.zeros(a_ref.shape, F32)
    lax.fori_loop(0, nfull // 2, body, 0)
    for jj in range(ndiag):
        if jj + 1 < ndiag:
            scores(nfull + jj + 1, (jj + 1) % 2, (jj + 1) * bk)
        softmax_pv(nfull + jj, jj % 2, jj * bk)

    dl = dl_ref[...].astype(F32)
    lam = (jnp.exp(jnp.sum(dl[0:1, :] * dl[1:2, :], axis=-1, keepdims=True))
           - jnp.exp(jnp.sum(dl[2:3, :] * dl[3:4, :], axis=-1, keepdims=True)) + lam_init)
    o1 = a_ref[0:dv, 0:bq] / a_ref[dv:dv + 1, 0:bq]
    o2 = a_ref[0:dv, bq:2 * bq] / a_ref[dv:dv + 1, bq:2 * bq]
    of = o1 - lam * o2
    of = of * lax.rsqrt(jnp.mean(of * of, axis=0, keepdims=True) + EPS) * sub_ref[...]
    o = (of * (1.0 - lam_init)).T
    o_ref[...] = (o * g_ref[...].astype(F32)).astype(BF16)


def _diff_attn(proj, dl, sub, lam_init, B, S, bq=2048, bk=512, cs=256):
    T = proj.shape[0]
    nq = S // bq
    H = DIFF_HEADS
    dv = DIFF_V_DIM
    per_seg = SEG // dv
    ones_rows = 2 * SUBLANES
    assert bq % (2 * bk) == 0 and S % bq == 0 and bk % cs == 0

    def qcol(seg):
        return pl.BlockSpec((bq, dv), lambda b, h, i: (b * nq + i, seg * per_seg + h))

    def kvcol(seg):
        return pl.BlockSpec((S, dv), lambda b, h, i: (b, seg * per_seg + h))

    est = (2 * 2 * S * dv * 2 + S * (dv + ones_rows) * 2 + 2 * 3 * bq * dv * 2
           + (dv + ones_rows + 1) * 2 * bq * 4 + 2 * bk * 2 * bq * 4 + cs * cs * 4 + 16 * bk * cs * 4
           + 6 * bq * dv * 4)
    return pl.pallas_call(
        functools.partial(_diff_kernel, lam_init=lam_init, cs=cs),
        grid=(B, H, nq),
        in_specs=[
            qcol(SEG_QD), kvcol(SEG_KD), kvcol(SEG_VD), qcol(SEG_GD),
            pl.BlockSpec((4, DIFF_HEAD_DIM), lambda b, h, i: (0, 0)),
            pl.BlockSpec((dv, 1), lambda b, h, i: (0, 0)),
        ],
        out_specs=pl.BlockSpec((bq, dv), lambda b, h, i: (b * nq + i, h)),
        out_shape=jax.ShapeDtypeStruct((T, DIFF_WIDTH), BF16),
        scratch_shapes=[
            pltpu.VMEM((S // bk, dv + ones_rows, bk), BF16),
            pltpu.VMEM((1, 2 * bq), F32),
            pltpu.VMEM((dv + ones_rows, 2 * bq), F32),
            pltpu.VMEM((2, bk, 2 * bq), F32),
            pltpu.VMEM((cs, cs), F32),
            pltpu.VMEM((2 * bq, dv), BF16),
        ],
        compiler_params=pltpu.CompilerParams(
            dimension_semantics=("parallel", "parallel", "arbitrary"),
            vmem_limit_bytes=_vmem_limit(est)),
        name="diff_attn",
    )(proj, proj, proj, proj, dl, sub)


def _merge_kernel(ya_ref, yb_ref, yc_ref, wa_ref, wb_ref, wc_ref, g0_ref, g1_ref, g2_ref, o_ref):
    for c in range(o_ref.shape[1] // MXU_DIM):
        cols = slice(c * MXU_DIM, (c + 1) * MXU_DIM)
        m = g0_ref[:, cols].astype(F32) * jnp.dot(ya_ref[...], wa_ref[:, cols],
                                                  preferred_element_type=F32)
        m = m + g1_ref[:, cols].astype(F32) * jnp.dot(yb_ref[...], wb_ref[:, cols],
                                                      preferred_element_type=F32)
        m = m + g2_ref[:, cols].astype(F32) * jnp.dot(yc_ref[...], wc_ref[:, cols],
                                                      preferred_element_type=F32)
        o_ref[:, cols] = m.astype(BF16)


def _merge(ya, yb, yc, wa, wb, wc, proj, bm=512):
    T, W = ya.shape
    D = wa.shape[1]
    ybk = pl.BlockSpec((bm, W), lambda i: (i, 0))
    wbk = pl.BlockSpec((W, D), lambda i: (0, 0), pipeline_mode=pl.Buffered(1))
    gm0 = SEG_GM * SEG // D

    def gate(br):
        return pl.BlockSpec((bm, D), lambda i: (i, gm0 + br))

    est = 2 * 3 * bm * W * 2 + 3 * W * D * 2 + 2 * 4 * bm * D * 2 + 8 * bm * MXU_DIM * 4
    return pl.pallas_call(
        _merge_kernel,
        grid=(T // bm,),
        in_specs=[ybk, ybk, ybk, wbk, wbk, wbk, gate(0), gate(1), gate(2)],
        out_specs=pl.BlockSpec((bm, D), lambda i: (i, 0)),
        out_shape=jax.ShapeDtypeStruct((T, D), BF16),
        compiler_params=pltpu.CompilerParams(
            dimension_semantics=("parallel",),
            vmem_limit_bytes=_vmem_limit(est)),
        name="merge",
    )(ya, yb, yc, wa, wb, wc, proj, proj, proj)


def _out_kernel(m_ref, w_ref, x_ref, g_ref, o_ref):
    for r in range(o_ref.shape[0] // OUT_ROWS):
        rows = slice(r * OUT_ROWS, (r + 1) * OUT_ROWS)
        o = jnp.dot(m_ref[rows, :], w_ref[...], preferred_element_type=F32)
        y = o * lax.rsqrt(jnp.mean(o * o, axis=-1, keepdims=True) + EPS)
        o_ref[rows, :] = x_ref[rows, :] + y * g_ref[...]


def _out_proj(m, w, x2, g, bm=512):
    T, D = x2.shape
    est = 2 * bm * D * 2 + D * D * 2 + 4 * bm * D * 4 + 6 * OUT_ROWS * D * 4
    return pl.pallas_call(
        _out_kernel,
        grid=(T // bm,),
        in_specs=[
            pl.BlockSpec((bm, D), lambda i: (i, 0)),
            pl.BlockSpec((D, D), lambda i: (0, 0), pipeline_mode=pl.Buffered(1)),
            pl.BlockSpec((bm, D), lambda i: (i, 0)),
            pl.BlockSpec((1, D), lambda i: (0, 0)),
        ],
        out_specs=pl.BlockSpec((bm, D), lambda i: (i, 0)),
        out_shape=jax.ShapeDtypeStruct((T, D), F32),
        compiler_params=pltpu.CompilerParams(
            dimension_semantics=("parallel",),
            vmem_limit_bytes=_vmem_limit(est)),
        name="out_proj",
    )(m, w, x2, g)


def _rope_tables(S):
    pos = jnp.arange(S, dtype=F32)
    ret_freq = 1.0 / (RET_ROT_BASE ** jnp.linspace(0.0, 1.0, RET_HEAD_DIM // 2, dtype=F32))
    ang = pos[:, None] * ret_freq[None, :]
    c, s = jnp.cos(ang), jnp.sin(ang)
    zero = jnp.zeros_like(s)
    rc = jnp.stack([c, c], axis=-1).reshape(S, RET_HEAD_DIM)
    rm = jnp.stack([-s, zero], axis=-1).reshape(S, RET_HEAD_DIM)
    rp = jnp.stack([zero, s], axis=-1).reshape(S, RET_HEAD_DIM)
    inv_freq = ROPE_THETA ** (-jnp.arange(0, ROPE_DIM, 2, dtype=F32) / ROPE_DIM)
    angd = pos[:, None] * inv_freq[None, :]
    cd, sd = jnp.cos(angd), jnp.sin(angd)
    half = ROPE_DIM // 2
    rest = DIFF_HEAD_DIM - ROPE_DIM
    one = jnp.ones((S, rest), F32)
    zero_r = jnp.zeros((S, rest), F32)
    zero_h = jnp.zeros((S, half), F32)
    da = jnp.concatenate([cd, cd, one], axis=-1)
    dm = jnp.concatenate([-sd, zero_h, zero_r], axis=-1)
    dp = jnp.concatenate([zero_h, sd, zero_r], axis=-1)
    rep = LANES // DIFF_HEAD_DIM
    return rc, rm, rp, jnp.tile(da, (1, rep)), jnp.tile(dm, (1, rep)), jnp.tile(dp, (1, rep))


def _decay_tables():
    C = RET_CHUNK
    H = RET_HEADS
    scale = RET_HEAD_DIM ** -0.5
    log_g = jnp.log1p(-jnp.exp2(-5.0 - jnp.arange(H, dtype=F32)))
    idx = jnp.arange(C, dtype=F32)
    rel = idx[:, None] - idx[None, :]
    intra = jnp.where(rel[None] >= 0,
                      jnp.exp(log_g[:, None, None] * jnp.maximum(rel, 0.0)[None]), 0.0) * scale
    k_decay = jnp.exp(log_g[:, None] * (C - 1.0 - idx)[None, :]) * scale
    q_decay = jnp.exp(log_g[:, None] * (idx + 1.0)[None, :])
    kdec = jnp.broadcast_to(k_decay[:, :, None], (H, C, RET_HEAD_DIM))
    qdec = jnp.broadcast_to(q_decay[:, :, None], (H, C, RET_HEAD_DIM))
    return intra, kdec, qdec


def _gate_weights(wa, wx):
    per = MXU_DIM // LRU_BLOCK
    nt = LRU_BLOCKS // per

    def dense(w):
        w4 = w.reshape(nt, per, LRU_BLOCK, LRU_BLOCK)
        eye = jnp.eye(per, dtype=w.dtype)
        d = jnp.einsum('tpde,pq->tpdqe', w4, eye)
        return d.reshape(nt, MXU_DIM, MXU_DIM)

    return jnp.concatenate([dense(wa), dense(wx)], axis=-1).astype(BF16)


def kernel(x, pre_norm, post_norm, w_in, conv_w, conv_b, lru_wa, lru_ba, lru_wx, lru_bx,
           lru_lambda, diff_lambda, diff_subln, w_branch_a, w_branch_b, w_branch_c, w_out):
    B, S, D = x.shape
    depth = w_in.shape[0]
    T = B * S
    tabs = _rope_tables(S)
    intra, kdec, qdec = _decay_tables()
    x2 = x.reshape(T, D)
    for l in range(depth):
        lam_init = 0.8 - 0.6 * math.exp(-0.3 * l)
        proj = _in_proj(x2, pre_norm[l][None, :], w_in[l].astype(BF16), tabs, S)
        ya = _lru(proj, conv_w[l], conv_b[l][None, :], _gate_weights(lru_wa[l], lru_wx[l]),
                  lru_ba[l][None, :], lru_bx[l][None, :], lru_lambda[l][None, :], B, S)
        yb = _retention(proj, intra, kdec, qdec, B, S)
        yc = _diff_attn(proj, diff_lambda[l], diff_subln[l][:, None], lam_init, B, S)
        m = _merge(ya, yb, yc, w_branch_a[l].astype(BF16), w_branch_b[l].astype(BF16),
                   w_branch_c[l].astype(BF16), proj)
        x2 = _out_proj(m, w_out[l].astype(BF16), x2, post_norm[l][None, :])
    return x2.reshape(B, S, D)
```

```python
import functools
import math

import numpy as np
import jax
import jax.numpy as jnp
from jax import lax
from jax.experimental import pallas as pl
from jax.experimental.pallas import tpu as pltpu

F32 = jnp.float32
BF16 = jnp.bfloat16

D_MODEL = 2048
LRU_WIDTH = 1024
LRU_BLOCKS = 16
LRU_BLOCK = LRU_WIDTH // LRU_BLOCKS
CONV_WIDTH = 4
LRU_C = 8.0
RET_HEADS = 8
RET_HEAD_DIM = 128
RET_WIDTH = RET_HEADS * RET_HEAD_DIM
RET_CHUNK = 128
RET_ROT_BASE = 10000.0
DIFF_HEADS = 8
DIFF_HEAD_DIM = 64
DIFF_V_DIM = 2 * DIFF_HEAD_DIM
DIFF_WIDTH = DIFF_HEADS * DIFF_V_DIM
ROPE_THETA = 500000.0
ROPE_DIM = DIFF_HEAD_DIM // 4
N_BRANCH = 3
EPS = 1e-6
D_IN = 2 * LRU_WIDTH + 4 * RET_WIDTH + 4 * DIFF_WIDTH + N_BRANCH * D_MODEL

LANES = 128
SUBLANES = 8
MXU_DIM = 256
VMEM_BYTES_V7X = 64 * 1024 * 1024

SEG = 1024
NORM_ROWS = 256
OUT_ROWS = 256
PROJ_ROWS = 256
SEG_XA, SEG_GA, SEG_QR, SEG_KR, SEG_VR, SEG_GR, SEG_QD, SEG_KD, SEG_VD, SEG_GD, SEG_GM = range(11)
SEG_KINDS = ("plain", "silu", "ret", "ret", "plain", "silu", "dif", "dif", "plain", "silu") + (
    "sigmoid",) * (N_BRANCH * D_MODEL // SEG)


def _sigmoid(x):
    return 1.0 / (1.0 + jnp.exp(-x))


def _vmem_limit(nbytes):
    return int(min(max(nbytes, 16 * 1024 * 1024), VMEM_BYTES_V7X - 8 * 1024 * 1024))


def _in_proj_kernel(x_ref, g_ref, w_ref, rc_ref, rm_ref, rp_ref, da_ref, dm_ref, dp_ref, o_ref, h_ref):
    j = pl.program_id(1)

    @pl.when(j == 0)
    def _():
        for r in range(x_ref.shape[0] // NORM_ROWS):
            rows = slice(r * NORM_ROWS, (r + 1) * NORM_ROWS)
            xf = x_ref[rows, :]
            ms = jnp.mean(xf * xf, axis=-1, keepdims=True)
            h_ref[rows, :] = ((xf * lax.rsqrt(ms + EPS)) * g_ref[...]).astype(BF16)

    def project(epilogues):
        bm, bn = o_ref.shape
        rsub = min(bm, PROJ_ROWS)
        for c in range(bn // MXU_DIM):
            cols = slice(c * MXU_DIM, (c + 1) * MXU_DIM)
            epilogue = epilogues[c * MXU_DIM // SEG]
            for r in range(bm // rsub):
                rows = slice(r * rsub, (r + 1) * rsub)
                acc = jnp.dot(h_ref[rows, :], w_ref[:, cols], preferred_element_type=F32)
                o_ref[rows, cols] = epilogue(acc, rows).astype(BF16)

    def per_lane_chunk(fn):
        def epilogue(acc, rows):
            return jnp.concatenate(
                [fn(acc[:, c * LANES:(c + 1) * LANES], rows) for c in range(MXU_DIM // LANES)],
                axis=1)
        return epilogue

    half = ROPE_DIM // 2
    epilogue_of = {
        "plain": lambda acc, rows: acc,
        "silu": lambda acc, rows: acc * _sigmoid(acc),
        "sigmoid": lambda acc, rows: _sigmoid(acc),
        "ret": per_lane_chunk(
            lambda xc, rows: xc * rc_ref[rows, :] + pltpu.roll(xc, LANES - 1, 1) * rm_ref[rows, :]
            + pltpu.roll(xc, 1, 1) * rp_ref[rows, :]),
        "dif": per_lane_chunk(
            lambda xc, rows: xc * da_ref[rows, :] + pltpu.roll(xc, LANES - half, 1) * dm_ref[rows, :]
            + pltpu.roll(xc, half, 1) * dp_ref[rows, :]),
    }
    nseg = o_ref.shape[1] // SEG
    blocks = {}
    for jv in range(len(SEG_KINDS) // nseg):
        blocks.setdefault(SEG_KINDS[jv * nseg:(jv + 1) * nseg], []).append(jv)
    for kinds, jvs in blocks.items():
        cond = functools.reduce(jnp.logical_or, [j == jv for jv in jvs])

        @pl.when(cond)
        def _(kinds=kinds):
            project([epilogue_of[kind] for kind in kinds])


def _in_proj(x2, g, w, tabs, S, bm=1024, bn=2 * SEG):
    T, D = x2.shape
    N = w.shape[1]
    assert bn % SEG == 0 and T % bm == 0 and S % bm == 0 and N % bn == 0
    nb = S // bm
    tab_spec = pl.BlockSpec((bm, LANES), lambda i, j: (i % nb, 0))
    est = (2 * bm * D * 4 + bm * D * 2 + 2 * D * bn * 2 + 2 * bm * bn * 2
           + 12 * bm * LANES * 4 + 8 * bm * MXU_DIM * 4 + 4 * NORM_ROWS * D * 4)
    return pl.pallas_call(
        _in_proj_kernel,
        grid=(T // bm, N // bn),
        in_specs=[
            pl.BlockSpec((bm, D), lambda i, j: (i, 0)),
            pl.BlockSpec((1, D), lambda i, j: (0, 0)),
            pl.BlockSpec((D, bn), lambda i, j: (0, j)),
            tab_spec, tab_spec, tab_spec, tab_spec, tab_spec, tab_spec,
        ],
        out_specs=pl.BlockSpec((bm, bn), lambda i, j: (i, j)),
        out_shape=jax.ShapeDtypeStruct((T, N), BF16),
        scratch_shapes=[pltpu.VMEM((bm, D), BF16)],
        compiler_params=pltpu.CompilerParams(
            dimension_semantics=("parallel", "arbitrary"),
            vmem_limit_bytes=_vmem_limit(est)),
        name="in_proj",
    )(x2, g, w, *tabs)


def _lru_kernel(xa_ref, ga_ref, cw_ref, cb_ref, wg_ref, ba_ref, bx_ref, lam_ref, o_ref,
                xext, a_s, u_s, hc):
    n = pl.program_id(1)
    nb, tc, _ = xa_ref.shape
    pad = SUBLANES

    @pl.when(n == 0)
    def _():
        xext[:, 0:pad, :] = jnp.zeros((nb, pad, LRU_WIDTH), F32)
        hc[...] = jnp.zeros_like(hc)

    z = -lam_ref[...]
    sp = jnp.maximum(z, 0.0) + jnp.log1p(jnp.exp(-jnp.abs(z)))
    gw = 2 * MXU_DIM
    for bi in range(nb):
        xext[bi, pad:pad + tc, :] = xa_ref[bi].astype(F32)
        xc = cb_ref[...] + cw_ref[CONV_WIDTH - 1:CONV_WIDTH, :] * xext[bi, pad:pad + tc, :]
        for k in range(CONV_WIDTH - 1):
            off = pad - (CONV_WIDTH - 1) + k
            xc = xc + cw_ref[k:k + 1, :] * xext[bi, off:off + tc, :]
        xext[bi, 0:pad, :] = xext[bi, tc:tc + pad, :]

        xb = xc.astype(BF16)
        for c in range(LRU_WIDTH // MXU_DIM):
            sl = slice(c * MXU_DIM, (c + 1) * MXU_DIM)
            g = jnp.dot(xb[:, sl], wg_ref[c], preferred_element_type=F32)
            r = _sigmoid(g[:, 0:MXU_DIM] + ba_ref[:, sl])
            ig = _sigmoid(g[:, MXU_DIM:gw] + bx_ref[:, sl])
            log_a = (-LRU_C * r) * sp[:, sl]
            a = jnp.exp(log_a)
            a_s[bi, :, sl] = a
            u_s[bi, :, sl] = jnp.sqrt(-jnp.tanh(log_a) * (a * a + 1.0)) * (ig * xc[:, sl])

    def step(t, hs):
        out = []
        for bi in range(nb):
            h = a_s[bi, pl.ds(t, 1), :] * hs[bi] + u_s[bi, pl.ds(t, 1), :]
            u_s[bi, pl.ds(t, 1), :] = h
            out.append(h)
        return tuple(out)

    hs = lax.fori_loop(0, tc, step, tuple(hc[bi] for bi in range(nb)), unroll=8)
    for bi in range(nb):
        hc[bi] = hs[bi]
        o_ref[bi] = (u_s[bi] * ga_ref[bi].astype(F32)).astype(BF16)


def _lru(proj, cw, cb, wg, ba, bx, lam, B, S, tc=256, nb=4):
    T = proj.shape[0]
    W = LRU_WIDTH
    assert B % nb == 0 and S % tc == 0
    proj3 = proj.reshape(B, S, proj.shape[1])
    vec = pl.BlockSpec((1, W), lambda b, n: (0, 0))
    est = 2 * 2 * nb * tc * W * 2 + 2 * nb * tc * W * 2 + 3 * nb * tc * W * 4 + 8 * tc * W * 4
    out = pl.pallas_call(
        _lru_kernel,
        grid=(B // nb, S // tc),
        in_specs=[
            pl.BlockSpec((nb, tc, W), lambda b, n: (b, n, SEG_XA)),
            pl.BlockSpec((nb, tc, W), lambda b, n: (b, n, SEG_GA)),
            pl.BlockSpec((CONV_WIDTH, W), lambda b, n: (0, 0)),
            vec,
            pl.BlockSpec((W // MXU_DIM, MXU_DIM, 2 * MXU_DIM), lambda b, n: (0, 0, 0)),
            vec, vec, vec,
        ],
        out_specs=pl.BlockSpec((nb, tc, W), lambda b, n: (b, n, 0)),
        out_shape=jax.ShapeDtypeStruct((B, S, W), BF16),
        scratch_shapes=[
            pltpu.VMEM((nb, tc + SUBLANES, W), F32),
            pltpu.VMEM((nb, tc, W), F32),
            pltpu.VMEM((nb, tc, W), F32),
            pltpu.VMEM((nb, 1, W), F32),
        ],
        compiler_params=pltpu.CompilerParams(
            dimension_semantics=("parallel", "arbitrary"),
            vmem_limit_bytes=_vmem_limit(est)),
        name="lru",
    )(proj3, proj3, cw, cb, wg, ba, bx, lam)
    return out.reshape(T, W)


def _ret_kernel(q_ref, k_ref, v_ref, g_ref, intra_ref, kdec_ref, qdec_ref, o_ref, st_ref):
    n = pl.program_id(1)
    tc = q_ref.shape[0]
    C = RET_CHUNK
    dh = RET_HEAD_DIM

    @pl.when(n == 0)
    def _():
        st_ref[...] = jnp.zeros_like(st_ref)

    nt_dims = (((1,), (1,)), ((), ()))
    for c in range(tc // C):
        rows = slice(c * C, (c + 1) * C)
        for h in range(RET_HEADS):
            sl = slice(h * dh, (h + 1) * dh)
            intra = intra_ref[h]
            kdec = kdec_ref[h]
            qdec = qdec_ref[h]
            cdec = qdec[C - 1:C, :]
            q = q_ref[rows, sl]
            k = k_ref[rows, sl]
            v = v_ref[rows, sl]
            state = st_ref[h]
            scores = lax.dot_general(q, k, nt_dims, preferred_element_type=F32) * intra
            inner = jnp.dot(scores.astype(BF16), v, preferred_element_type=F32)
            qd = (q.astype(F32) * qdec).astype(BF16)
            cross = jnp.dot(qd, state.astype(BF16), preferred_element_type=F32)
            kd = (k.astype(F32) * kdec).astype(BF16)
            kv = lax.dot_general(kd, v, (((0,), (0,)), ((), ())),
                                 preferred_element_type=F32)
            st_ref[h] = state * cdec + kv
            out = inner + cross
            mu = jnp.mean(out, axis=-1, keepdims=True)
            d = out - mu
            var = jnp.mean(d * d, axis=-1, keepdims=True)
            y = d * lax.rsqrt(var + EPS)
            o_ref[rows, sl] = (y * g_ref[rows, sl].astype(F32)).astype(BF16)


def _retention(proj, intra, kdec, qdec, B, S, tc=512):
    T = proj.shape[0]
    nt = S // tc
    W = RET_WIDTH
    C = RET_CHUNK

    def col(seg):
        return pl.BlockSpec((tc, W), lambda b, n: (b * nt + n, seg))

    est = 2 * 5 * tc * W * 2 + 2 * 3 * RET_HEADS * C * C * 4 + RET_HEADS * C * C * 4 + 16 * C * C * 4 * 8
    return pl.pallas_call(
        _ret_kernel,
        grid=(B, nt),
        in_specs=[
            col(SEG_QR), col(SEG_KR), col(SEG_VR), col(SEG_GR),
            pl.BlockSpec((RET_HEADS, C, C), lambda b, n: (0, 0, 0)),
            pl.BlockSpec((RET_HEADS, C, RET_HEAD_DIM), lambda b, n: (0, 0, 0)),
            pl.BlockSpec((RET_HEADS, C, RET_HEAD_DIM), lambda b, n: (0, 0, 0)),
        ],
        out_specs=pl.BlockSpec((tc, W), lambda b, n: (b * nt + n, 0)),
        out_shape=jax.ShapeDtypeStruct((T, W), BF16),
        scratch_shapes=[pltpu.VMEM((RET_HEADS, RET_HEAD_DIM, RET_HEAD_DIM), F32)],
        compiler_params=pltpu.CompilerParams(
            dimension_semantics=("parallel", "arbitrary"),
            vmem_limit_bytes=_vmem_limit(est)),
        name="retention",
    )(proj, proj, proj, proj, intra, kdec, qdec)


def _diff_kernel(q_ref, k_ref, v_ref, g_ref, dl_ref, sub_ref, o_ref, vxt_ref, m_ref, a_ref, s_ref,
                 tri_ref, qq_ref, *, lam_init, cs):
    qi = pl.program_id(2)
    bq = q_ref.shape[0]
    nkb, vx_rows, bk = vxt_ref.shape
    d = DIFF_HEAD_DIM
    dv = 2 * d
    nt_dims = (((1,), (1,)), ((), ()))

    @pl.when(qi == 0)
    def _():
        for c in range(nkb):
            vxt_ref[c, 0:dv, :] = v_ref[c * bk:(c + 1) * bk, :].astype(F32).T.astype(BF16)
            vxt_ref[c, dv:vx_rows, :] = jnp.ones((vx_rows - dv, bk), BF16)

    def prep_queries(c):
        lane = lax.broadcasted_iota(jnp.int32, (cs, dv), 1)
        q0 = (c * cs) % bq
        qs = q_ref[q0:q0 + cs, :].astype(F32) * (d ** -0.5 * math.log2(math.e))
        keep = (lane < d) if c * cs < bq else (lane >= d)
        qq_ref[c * cs:(c + 1) * cs, :] = jnp.where(keep, qs, 0.0).astype(BF16)

    ncol = 2 * bq // cs

    tri_ref[...] = jnp.where(lax.broadcasted_iota(jnp.int32, (cs, cs), 0)
                             <= lax.broadcasted_iota(jnp.int32, (cs, cs), 1), 0.0, -jnp.inf)

    def key_plan(c, kv0):
        nsubk = bk // cs
        if kv0 is None:
            return nsubk, False
        delta = (c * cs) % bq - kv0
        if delta < 0:
            return 0, False
        nfullk = min(nsubk, delta // cs)
        return nfullk, nfullk < nsubk

    def scores(t, slot, kv0=None, first=False):
        k = k_ref[pl.ds(pl.multiple_of(t * bk, bk), bk), :]
        for c in range(ncol):
            if first:
                prep_queries(c)
            nfullk, tri = key_plan(c, kv0)
            used = (nfullk + tri) * cs
            if used:
                s_ref[slot, 0:used, c * cs:(c + 1) * cs] = lax.dot_general(
                    k[0:used], qq_ref[c * cs:(c + 1) * cs, :], nt_dims,
                    preferred_element_type=F32)

    def softmax_pv(t, slot, kv0):
        vxt = vxt_ref[t]
        for c in range(ncol):
            nfullk, tri = key_plan(c, kv0)
            used = (nfullk + tri) * cs
            if not used:
                continue
            cols = slice(c * cs, (c + 1) * cs)
            parts = []
            if nfullk:
                parts.append(s_ref[slot, 0:nfullk * cs, cols])
            if tri:
                parts.append(s_ref[slot, nfullk * cs:used, cols] + tri_ref[...])
            m_new = m_prev = m_ref[:, cols]
            for s in parts:
                m_new = jnp.maximum(m_new, jnp.max(s, axis=0, keepdims=True))
            alpha = jnp.exp2(m_prev - m_new)
            p = [jnp.exp2(s - m_new).astype(BF16) for s in parts]
            p = p[0] if len(p) == 1 else jnp.concatenate(p, axis=0)
            pv = jnp.dot(vxt[:, 0:used], p, preferred_element_type=F32)
            a_ref[:, cols] = alpha * a_ref[:, cols] + pv
            m_ref[:, cols] = m_new

    ndiag = bq // bk
    nfull = qi * ndiag

    def body(u, carry):
        t = 2 * u
        scores(t + 1, 1)
        softmax_pv(t, 0, None)
        scores(t + 2, 0)
        softmax_pv(t + 1, 1, None)
        return carry

    scores(0, 0, first=True)
    m_ref[...] = jnp.full(m_ref.shape, -1e30, F32)
    a_ref[...] = jnp.zeros(a_ref.shape, F32)
    lax.fori_loop(0, nfull // 2, body, 0)
    for jj in range(ndiag):
        if jj + 1 < ndiag:
            scores(nfull + jj + 1, (jj + 1) % 2, (jj + 1) * bk)
        softmax_pv(nfull + jj, jj % 2, jj * bk)

    dl = dl_ref[...].astype(F32)
    lam = (jnp.exp(jnp.sum(dl[0:1, :] * dl[1:2, :], axis=-1, keepdims=True))
           - jnp.exp(jnp.sum(dl[2:3, :] * dl[3:4, :], axis=-1, keepdims=True)) + lam_init)
    o1 = a_ref[0:dv, 0:bq] / a_ref[dv:dv + 1, 0:bq]
    o2 = a_ref[0:dv, bq:2 * bq] / a_ref[dv:dv + 1, bq:2 * bq]
    of = o1 - lam * o2
    of = of * lax.rsqrt(jnp.mean(of * of, axis=0, keepdims=True) + EPS) * sub_ref[...]
    o = (of * (1.0 - lam_init)).T
    o_ref[...] = (o * g_ref[...].astype(F32)).astype(BF16)


def _diff_attn(proj, dl, sub, lam_init, B, S, bq=2048, bk=512, cs=256):
    T = proj.shape[0]
    nq = S // bq
    H = DIFF_HEADS
    dv = DIFF_V_DIM
    per_seg = SEG // dv
    ones_rows = 2 * SUBLANES
    assert bq % (2 * bk) == 0 and S % bq == 0 and bk % cs == 0

    def qcol(seg):
        return pl.BlockSpec((bq, dv), lambda b, h, i: (b * nq + i, seg * per_seg + h))

    def kvcol(seg):
        return pl.BlockSpec((S, dv), lambda b, h, i: (b, seg * per_seg + h))

    est = (2 * 2 * S * dv * 2 + S * (dv + ones_rows) * 2 + 2 * 3 * bq * dv * 2
           + (dv + ones_rows + 1) * 2 * bq * 4 + 2 * bk * 2 * bq * 4 + cs * cs * 4 + 16 * bk * cs * 4
           + 6 * bq * dv * 4)
    return pl.pallas_call(
        functools.partial(_diff_kernel, lam_init=lam_init, cs=cs),
        grid=(B, H, nq),
        in_specs=[
            qcol(SEG_QD), kvcol(SEG_KD), kvcol(SEG_VD), qcol(SEG_GD),
            pl.BlockSpec((4, DIFF_HEAD_DIM), lambda b, h, i: (0, 0)),
            pl.BlockSpec((dv, 1), lambda b, h, i: (0, 0)),
        ],
        out_specs=pl.BlockSpec((bq, dv), lambda b, h, i: (b * nq + i, h)),
        out_shape=jax.ShapeDtypeStruct((T, DIFF_WIDTH), BF16),
        scratch_shapes=[
            pltpu.VMEM((S // bk, dv + ones_rows, bk), BF16),
            pltpu.VMEM((1, 2 * bq), F32),
            pltpu.VMEM((dv + ones_rows, 2 * bq), F32),
            pltpu.VMEM((2, bk, 2 * bq), F32),
            pltpu.VMEM((cs, cs), F32),
            pltpu.VMEM((2 * bq, dv), BF16),
        ],
        compiler_params=pltpu.CompilerParams(
            dimension_semantics=("parallel", "parallel", "arbitrary"),
            vmem_limit_bytes=_vmem_limit(est)),
        name="diff_attn",
    )(proj, proj, proj, proj, dl, sub)


def _merge_kernel(ya_ref, yb_ref, yc_ref, wa_ref, wb_ref, wc_ref, g0_ref, g1_ref, g2_ref, o_ref):
    for c in range(o_ref.shape[1] // MXU_DIM):
        cols = slice(c * MXU_DIM, (c + 1) * MXU_DIM)
        m = g0_ref[:, cols].astype(F32) * jnp.dot(ya_ref[...], wa_ref[:, cols],
                                                  preferred_element_type=F32)
        m = m + g1_ref[:, cols].astype(F32) * jnp.dot(yb_ref[...], wb_ref[:, cols],
                                                      preferred_element_type=F32)
        m = m + g2_ref[:, cols].astype(F32) * jnp.dot(yc_ref[...], wc_ref[:, cols],
                                                      preferred_element_type=F32)
        o_ref[:, cols] = m.astype(BF16)


def _merge(ya, yb, yc, wa, wb, wc, proj, bm=512):
    T, W = ya.shape
    D = wa.shape[1]
    ybk = pl.BlockSpec((bm, W), lambda i: (i, 0))
    wbk = pl.BlockSpec((W, D), lambda i: (0, 0), pipeline_mode=pl.Buffered(1))
    gm0 = SEG_GM * SEG // D

    def gate(br):
        return pl.BlockSpec((bm, D), lambda i: (i, gm0 + br))

    est = 2 * 3 * bm * W * 2 + 3 * W * D * 2 + 2 * 4 * bm * D * 2 + 8 * bm * MXU_DIM * 4
    return pl.pallas_call(
        _merge_kernel,
        grid=(T // bm,),
        in_specs=[ybk, ybk, ybk, wbk, wbk, wbk, gate(0), gate(1), gate(2)],
        out_specs=pl.BlockSpec((bm, D), lambda i: (i, 0)),
        out_shape=jax.ShapeDtypeStruct((T, D), BF16),
        compiler_params=pltpu.CompilerParams(
            dimension_semantics=("parallel",),
            vmem_limit_bytes=_vmem_limit(est)),
        name="merge",
    )(ya, yb, yc, wa, wb, wc, proj, proj, proj)


def _out_kernel(m_ref, w_ref, x_ref, g_ref, o_ref):
    for r in range(o_ref.shape[0] // OUT_ROWS):
        rows = slice(r * OUT_ROWS, (r + 1) * OUT_ROWS)
        o = jnp.dot(m_ref[rows, :], w_ref[...], preferred_element_type=F32)
        y = o * lax.rsqrt(jnp.mean(o * o, axis=-1, keepdims=True) + EPS)
        o_ref[rows, :] = x_ref[rows, :] + y * g_ref[...]


def _out_proj(m, w, x2, g, bm=512):
    T, D = x2.shape
    est = 2 * bm * D * 2 + D * D * 2 + 4 * bm * D * 4 + 6 * OUT_ROWS * D * 4
    return pl.pallas_call(
        _out_kernel,
        grid=(T // bm,),
        in_specs=[
            pl.BlockSpec((bm, D), lambda i: (i, 0)),
            pl.BlockSpec((D, D), lambda i: (0, 0), pipeline_mode=pl.Buffered(1)),
            pl.BlockSpec((bm, D), lambda i: (i, 0)),
            pl.BlockSpec((1, D), lambda i: (0, 0)),
        ],
        out_specs=pl.BlockSpec((bm, D), lambda i: (i, 0)),
        out_shape=jax.ShapeDtypeStruct((T, D), F32),
        compiler_params=pltpu.CompilerParams(
            dimension_semantics=("parallel",),
            vmem_limit_bytes=_vmem_limit(est)),
        name="out_proj",
    )(m, w, x2, g)


def _rope_tables(S):
    pos = jnp.arange(S, dtype=F32)
    ret_freq = 1.0 / (RET_ROT_BASE ** jnp.linspace(0.0, 1.0, RET_HEAD_DIM // 2, dtype=F32))
    ang = pos[:, None] * ret_freq[None, :]
    c, s = jnp.cos(ang), jnp.sin(ang)
    zero = jnp.zeros_like(s)
    rc = jnp.stack([c, c], axis=-1).reshape(S, RET_HEAD_DIM)
    rm = jnp.stack([-s, zero], axis=-1).reshape(S, RET_HEAD_DIM)
    rp = jnp.stack([zero, s], axis=-1).reshape(S, RET_HEAD_DIM)
    inv_freq = ROPE_THETA ** (-jnp.arange(0, ROPE_DIM, 2, dtype=F32) / ROPE_DIM)
    angd = pos[:, None] * inv_freq[None, :]
    cd, sd = jnp.cos(angd), jnp.sin(angd)
    half = ROPE_DIM // 2
    rest = DIFF_HEAD_DIM - ROPE_DIM
    one = jnp.ones((S, rest), F32)
    zero_r = jnp.zeros((S, rest), F32)
    zero_h = jnp.zeros((S, half), F32)
    da = jnp.concatenate([cd, cd, one], axis=-1)
    dm = jnp.concatenate([-sd, zero_h, zero_r], axis=-1)
    dp = jnp.concatenate([zero_h, sd, zero_r], axis=-1)
    rep = LANES // DIFF_HEAD_DIM
    return rc, rm, rp, jnp.tile(da, (1, rep)), jnp.tile(dm, (1, rep)), jnp.tile(dp, (1, rep))


def _decay_tables():
    C = RET_CHUNK
    H = RET_HEADS
    scale = RET_HEAD_DIM ** -0.5
    log_g = jnp.log1p(-jnp.exp2(-5.0 - jnp.arange(H, dtype=F32)))
    idx = jnp.arange(C, dtype=F32)
    rel = idx[:, None] - idx[None, :]
    intra = jnp.where(rel[None] >= 0,
                      jnp.exp(log_g[:, None, None] * jnp.maximum(rel, 0.0)[None]), 0.0) * scale
    k_decay = jnp.exp(log_g[:, None] * (C - 1.0 - idx)[None, :]) * scale
    q_decay = jnp.exp(log_g[:, None] * (idx + 1.0)[None, :])
    kdec = jnp.broadcast_to(k_decay[:, :, None], (H, C, RET_HEAD_DIM))
    qdec = jnp.broadcast_to(q_decay[:, :, None], (H, C, RET_HEAD_DIM))
    return intra, kdec, qdec


def _gate_weights(wa, wx):
    per = MXU_DIM // LRU_BLOCK
    nt = LRU_BLOCKS // per

    def dense(w):
        w4 = w.reshape(nt, per, LRU_BLOCK, LRU_BLOCK)
        eye = jnp.eye(per, dtype=w.dtype)
        d = jnp.einsum('tpde,pq->tpdqe', w4, eye)
        return d.reshape(nt, MXU_DIM, MXU_DIM)

    return jnp.concatenate([dense(wa), dense(wx)], axis=-1).astype(BF16)


def _cast_kernel(w_ref, o_ref):
    o_ref[...] = w_ref[...].astype(BF16)


def _cast_layer_weight(w, l, br=256, bc=4096):
    _, R, C = w.shape
    br, bc = min(br, R), min(bc, C)
    assert R % br == 0 and C % bc == 0
    return pl.pallas_call(
        _cast_kernel,
        grid=(R // br, C // bc),
        in_specs=[pl.BlockSpec((None, br, bc), lambda i, j: (l, i, j))],
        out_specs=pl.BlockSpec((br, bc), lambda i, j: (i, j)),
        out_shape=jax.ShapeDtypeStruct((R, C), BF16),
        compiler_params=pltpu.CompilerParams(
            dimension_semantics=("parallel", "parallel"),
            vmem_limit_bytes=_vmem_limit(2 * br * bc * (4 + 2) + br * bc * 4)),
        name="cast_weight",
    )(w)


def kernel(x, pre_norm, post_norm, w_in, conv_w, conv_b, lru_wa, lru_ba, lru_wx, lru_bx,
           lru_lambda, diff_lambda, diff_subln, w_branch_a, w_branch_b, w_branch_c, w_out):
    B, S, D = x.shape
    depth = w_in.shape[0]
    T = B * S
    tabs = _rope_tables(S)
    intra, kdec, qdec = _decay_tables()
    x2 = x.reshape(T, D)
    for l in range(depth):
        lam_init = 0.8 - 0.6 * math.exp(-0.3 * l)
        proj = _in_proj(x2, pre_norm[l][None, :], _cast_layer_weight(w_in, l), tabs, S)
        ya = _lru(proj, conv_w[l], conv_b[l][None, :], _gate_weights(lru_wa[l], lru_wx[l]),
                  lru_ba[l][None, :], lru_bx[l][None, :], lru_lambda[l][None, :], B, S)
        yb = _retention(proj, intra, kdec, qdec, B, S)
        yc = _diff_attn(proj, diff_lambda[l], diff_subln[l][:, None], lam_init, B, S)
        m = _merge(ya, yb, yc, w_branch_a[l].astype(BF16), w_branch_b[l].astype(BF16),
                   w_branch_c[l].astype(BF16), proj)
        x2 = _out_proj(m, w_out[l].astype(BF16), x2, post_norm[l][None, :])
    return x2.reshape(B, S, D)
```

```python
import functools
import math

import jax
import jax.numpy as jnp
from jax import lax
from jax.experimental import pallas as pl
from jax.experimental.pallas import tpu as pltpu

F32 = jnp.float32
BF16 = jnp.bfloat16

D_MODEL = 2048
LRU_WIDTH = 1024
LRU_BLOCKS = 16
LRU_BLOCK = LRU_WIDTH // LRU_BLOCKS
CONV_WIDTH = 4
LRU_C = 8.0
RET_HEADS = 8
RET_HEAD_DIM = 128
RET_WIDTH = RET_HEADS * RET_HEAD_DIM
RET_CHUNK = 128
RET_ROT_BASE = 10000.0
DIFF_HEADS = 8
DIFF_HEAD_DIM = 64
DIFF_V_DIM = 2 * DIFF_HEAD_DIM
DIFF_WIDTH = DIFF_HEADS * DIFF_V_DIM
ROPE_THETA = 500000.0
ROPE_DIM = DIFF_HEAD_DIM // 4
N_BRANCH = 3
EPS = 1e-6

LANES = 128
SUBLANES = 8
MXU_DIM = 256
VMEM_BYTES_V7X = 64 * 1024 * 1024
VMEM_RESERVE = 8 * 1024 * 1024
VMEM_MIN_REQUEST = 16 * 1024 * 1024

SEG = 1024
NORM_ROWS = 256
OUT_ROWS = 256
PROJ_ROWS = 256
LOOP_BLOCKS = 4
SEG_XA, SEG_GA, SEG_QR, SEG_KR, SEG_VR, SEG_GR, SEG_QD, SEG_KD, SEG_VD, SEG_GD, SEG_GM = range(11)
SEG_KINDS = ("plain", "silu", "ret", "ret", "plain", "silu", "dif", "dif", "plain", "silu") + (
    "sigmoid",) * (N_BRANCH * D_MODEL // SEG)


def _sigmoid(x):
    return 1.0 / (1.0 + jnp.exp(-x))


def _vmem_limit(nbytes):
    return int(min(max(nbytes, VMEM_MIN_REQUEST), VMEM_BYTES_V7X - VMEM_RESERVE))


def _in_proj_kernel(x_ref, g_ref, w_ref, rc_ref, rm_ref, rp_ref, da_ref, dm_ref, dp_ref, o_ref, h_ref):
    j = pl.program_id(1)

    def normalise():
        for r in range(x_ref.shape[0] // NORM_ROWS):
            rows = slice(r * NORM_ROWS, (r + 1) * NORM_ROWS)
            xf = x_ref[rows, :]
            ms = jnp.mean(xf * xf, axis=-1, keepdims=True)
            h_ref[rows, :] = ((xf * lax.rsqrt(ms + EPS)) * g_ref[...]).astype(BF16)

    def project(epilogues):
        bm, bn = o_ref.shape
        rsub = min(bm, PROJ_ROWS)
        for c in range(bn // MXU_DIM):
            cols = slice(c * MXU_DIM, (c + 1) * MXU_DIM)
            epilogue = epilogues[c * MXU_DIM // SEG]
            for r in range(bm // rsub):
                rows = slice(r * rsub, (r + 1) * rsub)
                acc = jnp.dot(h_ref[rows, :], w_ref[:, cols], preferred_element_type=F32)
                o_ref[rows, cols] = epilogue(acc, rows).astype(BF16)

    def per_lane_chunk(fn):
        def epilogue(acc, rows):
            return jnp.concatenate(
                [fn(acc[:, c * LANES:(c + 1) * LANES], rows) for c in range(MXU_DIM // LANES)],
                axis=1)
        return epilogue

    half = ROPE_DIM // 2
    epilogue_of = {
        "plain": lambda acc, rows: acc,
        "silu": lambda acc, rows: acc * _sigmoid(acc),
        "sigmoid": lambda acc, rows: _sigmoid(acc),
        "ret": per_lane_chunk(
            lambda xc, rows: xc * rc_ref[rows, :] + pltpu.roll(xc, LANES - 1, 1) * rm_ref[rows, :]
            + pltpu.roll(xc, 1, 1) * rp_ref[rows, :]),
        "dif": per_lane_chunk(
            lambda xc, rows: xc * da_ref[rows, :] + pltpu.roll(xc, LANES - half, 1) * dm_ref[rows, :]
            + pltpu.roll(xc, half, 1) * dp_ref[rows, :]),
    }
    nseg = o_ref.shape[1] // SEG

    @pl.when(j == 0)
    def _():
        normalise()
        project([epilogue_of[kind] for kind in SEG_KINDS[0:nseg]])

    blocks = {}
    for jv in range(1, len(SEG_KINDS) // nseg):
        blocks.setdefault(SEG_KINDS[jv * nseg:(jv + 1) * nseg], []).append(jv)
    for kinds, jvs in blocks.items():
        cond = functools.reduce(jnp.logical_or, [j == jv for jv in jvs])

        @pl.when(cond)
        def _(kinds=kinds):
            project([epilogue_of[kind] for kind in kinds])


def _in_proj(x2, g, w, tabs, S, bm=1024, bn=2 * SEG):
    T, D = x2.shape
    N = w.shape[1]
    assert bn % SEG == 0 and T % bm == 0 and S % bm == 0 and N % bn == 0
    nb = S // bm
    tab_spec = pl.BlockSpec((bm, LANES), lambda i, j: (i % nb, 0))
    est = (2 * bm * D * 4 + bm * D * 2 + 2 * D * bn * 2 + 2 * bm * bn * 2
           + 12 * bm * LANES * 4 + 8 * bm * MXU_DIM * 4 + 4 * NORM_ROWS * D * 4)
    return pl.pallas_call(
        _in_proj_kernel,
        grid=(T // bm, N // bn),
        in_specs=[
            pl.BlockSpec((bm, D), lambda i, j: (i, 0)),
            pl.BlockSpec((1, D), lambda i, j: (0, 0)),
            pl.BlockSpec((D, bn), lambda i, j: (0, j)),
            tab_spec, tab_spec, tab_spec, tab_spec, tab_spec, tab_spec,
        ],
        out_specs=pl.BlockSpec((bm, bn), lambda i, j: (i, j)),
        out_shape=jax.ShapeDtypeStruct((T, N), BF16),
        scratch_shapes=[pltpu.VMEM((bm, D), BF16)],
        compiler_params=pltpu.CompilerParams(
            dimension_semantics=("parallel", "arbitrary"),
            vmem_limit_bytes=_vmem_limit(est)),
        name="in_proj",
    )(x2, g, w, *tabs)


def _lru_kernel(xa_ref, ga_ref, cw_ref, cb_ref, wg_ref, ba_ref, bx_ref, lam_ref, o_ref,
                xext, a_s, u_s, hc):
    n = pl.program_id(1)
    nb, tc, _ = xa_ref.shape
    pad = SUBLANES

    @pl.when(n == 0)
    def _():
        xext[:, 0:pad, :] = jnp.zeros((nb, pad, LRU_WIDTH), F32)
        hc[...] = jnp.zeros_like(hc)

    z = -lam_ref[...]
    sp = jnp.maximum(z, 0.0) + jnp.log1p(jnp.exp(-jnp.abs(z)))
    gw = 2 * MXU_DIM
    for bi in range(nb):
        xext[bi, pad:pad + tc, :] = xa_ref[bi].astype(F32)
        xc = cb_ref[...] + cw_ref[CONV_WIDTH - 1:CONV_WIDTH, :] * xext[bi, pad:pad + tc, :]
        for k in range(CONV_WIDTH - 1):
            off = pad - (CONV_WIDTH - 1) + k
            xc = xc + cw_ref[k:k + 1, :] * xext[bi, off:off + tc, :]
        xext[bi, 0:pad, :] = xext[bi, tc:tc + pad, :]

        xb = xc.astype(BF16)
        for c in range(LRU_WIDTH // MXU_DIM):
            sl = slice(c * MXU_DIM, (c + 1) * MXU_DIM)
            g = jnp.dot(xb[:, sl], wg_ref[c], preferred_element_type=F32)
            r = _sigmoid(g[:, 0:MXU_DIM] + ba_ref[:, sl])
            ig = _sigmoid(g[:, MXU_DIM:gw] + bx_ref[:, sl])
            log_a = (-LRU_C * r) * sp[:, sl]
            a = jnp.exp(log_a)
            a_s[bi, :, sl] = a
            u_s[bi, :, sl] = jnp.sqrt(-jnp.tanh(log_a) * (a * a + 1.0)) * (ig * xc[:, sl])

    def step(t, hs):
        out = []
        for bi in range(nb):
            h = a_s[bi, pl.ds(t, 1), :] * hs[bi] + u_s[bi, pl.ds(t, 1), :]
            u_s[bi, pl.ds(t, 1), :] = h
            out.append(h)
        return tuple(out)

    hs = lax.fori_loop(0, tc, step, tuple(hc[bi] for bi in range(nb)), unroll=8)
    for bi in range(nb):
        hc[bi] = hs[bi]
        o_ref[bi] = (u_s[bi] * ga_ref[bi].astype(F32)).astype(BF16)


def _lru(proj, cw, cb, wg, ba, bx, lam, B, S, tc=256, nb=4):
    T = proj.shape[0]
    W = LRU_WIDTH
    assert B % nb == 0 and S % tc == 0
    proj3 = proj.reshape(B, S, proj.shape[1])
    vec = pl.BlockSpec((1, W), lambda b, n: (0, 0))
    est = 2 * 2 * nb * tc * W * 2 + 2 * nb * tc * W * 2 + 3 * nb * tc * W * 4 + 8 * tc * W * 4
    out = pl.pallas_call(
        _lru_kernel,
        grid=(B // nb, S // tc),
        in_specs=[
            pl.BlockSpec((nb, tc, W), lambda b, n: (b, n, SEG_XA)),
            pl.BlockSpec((nb, tc, W), lambda b, n: (b, n, SEG_GA)),
            pl.BlockSpec((CONV_WIDTH, W), lambda b, n: (0, 0)),
            vec,
            pl.BlockSpec((W // MXU_DIM, MXU_DIM, 2 * MXU_DIM), lambda b, n: (0, 0, 0)),
            vec, vec, vec,
        ],
        out_specs=pl.BlockSpec((nb, tc, W), lambda b, n: (b, n, 0)),
        out_shape=jax.ShapeDtypeStruct((B, S, W), BF16),
        scratch_shapes=[
            pltpu.VMEM((nb, tc + SUBLANES, W), F32),
            pltpu.VMEM((nb, tc, W), F32),
            pltpu.VMEM((nb, tc, W), F32),
            pltpu.VMEM((nb, 1, W), F32),
        ],
        compiler_params=pltpu.CompilerParams(
            dimension_semantics=("parallel", "arbitrary"),
            vmem_limit_bytes=_vmem_limit(est)),
        name="lru",
    )(proj3, proj3, cw, cb, wg, ba, bx, lam)
    return out.reshape(T, W)


def _ret_kernel(q_ref, k_ref, v_ref, g_ref, intra_ref, kdec_ref, qdec_ref, o_ref, st_ref):
    n = pl.program_id(1)
    tc = q_ref.shape[0]
    C = RET_CHUNK
    dh = RET_HEAD_DIM

    @pl.when(n == 0)
    def _():
        st_ref[...] = jnp.zeros_like(st_ref)

    nt_dims = (((1,), (1,)), ((), ()))
    for c in range(tc // C):
        rows = slice(c * C, (c + 1) * C)
        for h in range(RET_HEADS):
            sl = slice(h * dh, (h + 1) * dh)
            intra = intra_ref[h]
            kdec = kdec_ref[h]
            qdec = qdec_ref[h]
            cdec = qdec[C - 1:C, :]
            q = q_ref[rows, sl]
            k = k_ref[rows, sl]
            v = v_ref[rows, sl]
            state = st_ref[h]
            scores = lax.dot_general(q, k, nt_dims, preferred_element_type=F32) * intra
            inner = jnp.dot(scores.astype(BF16), v, preferred_element_type=F32)
            qd = (q.astype(F32) * qdec).astype(BF16)
            cross = jnp.dot(qd, state.astype(BF16), preferred_element_type=F32)
            kd = (k.astype(F32) * kdec).astype(BF16)
            kv = lax.dot_general(kd, v, (((0,), (0,)), ((), ())),
                                 preferred_element_type=F32)
            st_ref[h] = state * cdec + kv
            out = inner + cross
            mu = jnp.mean(out, axis=-1, keepdims=True)
            d = out - mu
            var = jnp.mean(d * d, axis=-1, keepdims=True)
            y = d * lax.rsqrt(var + EPS)
            o_ref[rows, sl] = (y * g_ref[rows, sl].astype(F32)).astype(BF16)


def _retention(proj, intra, kdec, qdec, B, S, tc=512):
    T = proj.shape[0]
    nt = S // tc
    W = RET_WIDTH
    C = RET_CHUNK

    def col(seg):
        return pl.BlockSpec((tc, W), lambda b, n: (b * nt + n, seg))

    est = 2 * 5 * tc * W * 2 + 2 * 3 * RET_HEADS * C * C * 4 + RET_HEADS * C * C * 4 + 16 * C * C * 4 * 8
    return pl.pallas_call(
        _ret_kernel,
        grid=(B, nt),
        in_specs=[
            col(SEG_QR), col(SEG_KR), col(SEG_VR), col(SEG_GR),
            pl.BlockSpec((RET_HEADS, C, C), lambda b, n: (0, 0, 0)),
            pl.BlockSpec((RET_HEADS, C, RET_HEAD_DIM), lambda b, n: (0, 0, 0)),
            pl.BlockSpec((RET_HEADS, C, RET_HEAD_DIM), lambda b, n: (0, 0, 0)),
        ],
        out_specs=pl.BlockSpec((tc, W), lambda b, n: (b * nt + n, 0)),
        out_shape=jax.ShapeDtypeStruct((T, W), BF16),
        scratch_shapes=[pltpu.VMEM((RET_HEADS, RET_HEAD_DIM, RET_HEAD_DIM), F32)],
        compiler_params=pltpu.CompilerParams(
            dimension_semantics=("parallel", "arbitrary"),
            vmem_limit_bytes=_vmem_limit(est)),
        name="retention",
    )(proj, proj, proj, proj, intra, kdec, qdec)


def _diff_kernel(q_ref, k_ref, v_ref, g_ref, dl_ref, sub_ref, o_ref, vxt_ref, m_ref, a_ref, s_ref,
                 tri_ref, qq_ref, *, lam_init, cs):
    qi = pl.program_id(2)
    bq = q_ref.shape[0]
    nkb, vx_rows, bk = vxt_ref.shape
    d = DIFF_HEAD_DIM
    dv = 2 * d
    nt_dims = (((1,), (1,)), ((), ()))

    @pl.when(qi == 0)
    def _():
        for c in range(nkb):
            vxt_ref[c, 0:dv, :] = v_ref[c * bk:(c + 1) * bk, :].astype(F32).T.astype(BF16)
            vxt_ref[c, dv:vx_rows, :] = jnp.ones((vx_rows - dv, bk), BF16)

    def prep_queries(c):
        lane = lax.broadcasted_iota(jnp.int32, (cs, dv), 1)
        q0 = (c * cs) % bq
        qs = q_ref[q0:q0 + cs, :].astype(F32) * (d ** -0.5 * math.log2(math.e))
        keep = (lane < d) if c * cs < bq else (lane >= d)
        qq_ref[c * cs:(c + 1) * cs, :] = jnp.where(keep, qs, 0.0).astype(BF16)

    ncol = 2 * bq // cs

    tri_ref[...] = jnp.where(lax.broadcasted_iota(jnp.int32, (cs, cs), 0)
                             <= lax.broadcasted_iota(jnp.int32, (cs, cs), 1), 0.0, -jnp.inf)

    def key_plan(c, kv0):
        nsubk = bk // cs
        if kv0 is None:
            return nsubk, False
        delta = (c * cs) % bq - kv0
        if delta < 0:
            return 0, False
        nfullk = min(nsubk, delta // cs)
        return nfullk, nfullk < nsubk

    def scores(t, slot, kv0=None, first=False):
        k = k_ref[pl.ds(pl.multiple_of(t * bk, bk), bk), :]
        for c in range(ncol):
            if first:
                prep_queries(c)
            nfullk, tri = key_plan(c, kv0)
            used = (nfullk + tri) * cs
            if used:
                s_ref[slot, 0:used, c * cs:(c + 1) * cs] = lax.dot_general(
                    k[0:used], qq_ref[c * cs:(c + 1) * cs, :], nt_dims,
                    preferred_element_type=F32)

    def softmax_pv(t, slot, kv0):
        vxt = vxt_ref[t]
        for c in range(ncol):
            nfullk, tri = key_plan(c, kv0)
            used = (nfullk + tri) * cs
            if not used:
                continue
            cols = slice(c * cs, (c + 1) * cs)
            parts = []
            if nfullk:
                parts.append(s_ref[slot, 0:nfullk * cs, cols])
            if tri:
                parts.append(s_ref[slot, nfullk * cs:used, cols] + tri_ref[...])
            m_new = m_prev = m_ref[:, cols]
            for s in parts:
                m_new = jnp.maximum(m_new, jnp.max(s, axis=0, keepdims=True))
            alpha = jnp.exp2(m_prev - m_new)
            p = [jnp.exp2(s - m_new).astype(BF16) for s in parts]
            p = p[0] if len(p) == 1 else jnp.concatenate(p, axis=0)
            pv = jnp.dot(vxt[:, 0:used], p, preferred_element_type=F32)
            a_ref[:, cols] = alpha * a_ref[:, cols] + pv
            m_ref[:, cols] = m_new

    ndiag = bq // bk
    nfull = qi * ndiag

    def body(u, carry):
        for i in range(LOOP_BLOCKS):
            t = LOOP_BLOCKS * u + i
            scores(t + 1, (i + 1) % 2)
            softmax_pv(t, i % 2, None)
        return carry

    scores(0, 0, first=True)
    m_ref[...] = jnp.full(m_ref.shape, -1e30, F32)
    a_ref[...] = jnp.zeros(a_ref.shape, F32)
    lax.fori_loop(0, nfull // LOOP_BLOCKS, body, 0)
    for jj in range(ndiag):
        if jj + 1 < ndiag:
            scores(nfull + jj + 1, (jj + 1) % 2, (jj + 1) * bk)
        softmax_pv(nfull + jj, jj % 2, jj * bk)

    dl = dl_ref[...].astype(F32)
    lam = (jnp.exp(jnp.sum(dl[0:1, :] * dl[1:2, :], axis=-1, keepdims=True))
           - jnp.exp(jnp.sum(dl[2:3, :] * dl[3:4, :], axis=-1, keepdims=True)) + lam_init)
    o1 = a_ref[0:dv, 0:bq] / a_ref[dv:dv + 1, 0:bq]
    o2 = a_ref[0:dv, bq:2 * bq] / a_ref[dv:dv + 1, bq:2 * bq]
    of = o1 - lam * o2
    of = of * lax.rsqrt(jnp.mean(of * of, axis=0, keepdims=True) + EPS) * sub_ref[...]
    o = (of * (1.0 - lam_init)).T
    o_ref[...] = (o * g_ref[...].astype(F32)).astype(BF16)


def _diff_attn(proj, dl, sub, lam_init, B, S, bq=2048, bk=512, cs=256):
    T = proj.shape[0]
    nq = S // bq
    H = DIFF_HEADS
    dv = DIFF_V_DIM
    per_seg = SEG // dv
    ones_rows = 2 * SUBLANES
    assert bq % (LOOP_BLOCKS * bk) == 0 and LOOP_BLOCKS % 2 == 0 and S % bq == 0 and bk % cs == 0

    def qcol(seg):
        return pl.BlockSpec((bq, dv), lambda b, h, i: (b * nq + i, seg * per_seg + h))

    def kvcol(seg):
        return pl.BlockSpec((S, dv), lambda b, h, i: (b, seg * per_seg + h))

    est = (2 * 2 * S * dv * 2 + S * (dv + ones_rows) * 2 + 2 * 3 * bq * dv * 2
           + (dv + ones_rows + 1) * 2 * bq * 4 + 2 * bk * 2 * bq * 4 + cs * cs * 4 + 8 * LOOP_BLOCKS * bk * cs * 4
           + 6 * bq * dv * 4)
    return pl.pallas_call(
        functools.partial(_diff_kernel, lam_init=lam_init, cs=cs),
        grid=(B, H, nq),
        in_specs=[
            qcol(SEG_QD), kvcol(SEG_KD), kvcol(SEG_VD), qcol(SEG_GD),
            pl.BlockSpec((4, DIFF_HEAD_DIM), lambda b, h, i: (0, 0)),
            pl.BlockSpec((dv, 1), lambda b, h, i: (0, 0)),
        ],
        out_specs=pl.BlockSpec((bq, dv), lambda b, h, i: (b * nq + i, h)),
        out_shape=jax.ShapeDtypeStruct((T, DIFF_WIDTH), BF16),
        scratch_shapes=[
            pltpu.VMEM((S // bk, dv + ones_rows, bk), BF16),
            pltpu.VMEM((1, 2 * bq), F32),
            pltpu.VMEM((dv + ones_rows, 2 * bq), F32),
            pltpu.VMEM((2, bk, 2 * bq), F32),
            pltpu.VMEM((cs, cs), F32),
            pltpu.VMEM((2 * bq, dv), BF16),
        ],
        compiler_params=pltpu.CompilerParams(
            dimension_semantics=("parallel", "parallel", "arbitrary"),
            vmem_limit_bytes=_vmem_limit(est)),
        name="diff_attn",
    )(proj, proj, proj, proj, dl, sub)


def _merge_kernel(ya_ref, yb_ref, yc_ref, wa_ref, wb_ref, wc_ref, g0_ref, g1_ref, g2_ref, o_ref):
    for c in range(o_ref.shape[1] // MXU_DIM):
        cols = slice(c * MXU_DIM, (c + 1) * MXU_DIM)
        m = g0_ref[:, cols].astype(F32) * jnp.dot(ya_ref[...], wa_ref[:, cols],
                                                  preferred_element_type=F32)
        m = m + g1_ref[:, cols].astype(F32) * jnp.dot(yb_ref[...], wb_ref[:, cols],
                                                      preferred_element_type=F32)
        m = m + g2_ref[:, cols].astype(F32) * jnp.dot(yc_ref[...], wc_ref[:, cols],
                                                      preferred_element_type=F32)
        o_ref[:, cols] = m.astype(BF16)


def _merge(ya, yb, yc, wa, wb, wc, proj, bm=512):
    T, W = ya.shape
    D = wa.shape[1]
    ybk = pl.BlockSpec((bm, W), lambda i: (i, 0))
    wbk = pl.BlockSpec((W, D), lambda i: (0, 0), pipeline_mode=pl.Buffered(1))
    gm0 = SEG_GM * SEG // D

    def gate(br):
        return pl.BlockSpec((bm, D), lambda i: (i, gm0 + br))

    est = 2 * 3 * bm * W * 2 + 3 * W * D * 2 + 2 * 4 * bm * D * 2 + 8 * bm * MXU_DIM * 4
    return pl.pallas_call(
        _merge_kernel,
        grid=(T // bm,),
        in_specs=[ybk, ybk, ybk, wbk, wbk, wbk, gate(0), gate(1), gate(2)],
        out_specs=pl.BlockSpec((bm, D), lambda i: (i, 0)),
        out_shape=jax.ShapeDtypeStruct((T, D), BF16),
        compiler_params=pltpu.CompilerParams(
            dimension_semantics=("parallel",),
            vmem_limit_bytes=_vmem_limit(est)),
        name="merge",
    )(ya, yb, yc, wa, wb, wc, proj, proj, proj)


def _out_kernel(m_ref, w_ref, x_ref, g_ref, o_ref):
    for r in range(o_ref.shape[0] // OUT_ROWS):
        rows = slice(r * OUT_ROWS, (r + 1) * OUT_ROWS)
        o = jnp.dot(m_ref[rows, :], w_ref[...], preferred_element_type=F32)
        y = o * lax.rsqrt(jnp.mean(o * o, axis=-1, keepdims=True) + EPS)
        o_ref[rows, :] = x_ref[rows, :] + y * g_ref[...]


def _out_proj(m, w, x2, g, bm=512):
    T, D = x2.shape
    est = 2 * bm * D * 2 + D * D * 2 + 4 * bm * D * 4 + 6 * OUT_ROWS * D * 4
    return pl.pallas_call(
        _out_kernel,
        grid=(T // bm,),
        in_specs=[
            pl.BlockSpec((bm, D), lambda i: (i, 0)),
            pl.BlockSpec((D, D), lambda i: (0, 0), pipeline_mode=pl.Buffered(1)),
            pl.BlockSpec((bm, D), lambda i: (i, 0)),
            pl.BlockSpec((1, D), lambda i: (0, 0)),
        ],
        out_specs=pl.BlockSpec((bm, D), lambda i: (i, 0)),
        out_shape=jax.ShapeDtypeStruct((T, D), F32),
        compiler_params=pltpu.CompilerParams(
            dimension_semantics=("parallel",),
            vmem_limit_bytes=_vmem_limit(est)),
        name="out_proj",
    )(m, w, x2, g)


def _rope_tables(S):
    pos = jnp.arange(S, dtype=F32)
    ret_freq = 1.0 / (RET_ROT_BASE ** jnp.linspace(0.0, 1.0, RET_HEAD_DIM // 2, dtype=F32))
    ang = pos[:, None] * ret_freq[None, :]
    c, s = jnp.cos(ang), jnp.sin(ang)
    zero = jnp.zeros_like(s)
    rc = jnp.stack([c, c], axis=-1).reshape(S, RET_HEAD_DIM)
    rm = jnp.stack([-s, zero], axis=-1).reshape(S, RET_HEAD_DIM)
    rp = jnp.stack([zero, s], axis=-1).reshape(S, RET_HEAD_DIM)
    inv_freq = ROPE_THETA ** (-jnp.arange(0, ROPE_DIM, 2, dtype=F32) / ROPE_DIM)
    angd = pos[:, None] * inv_freq[None, :]
    cd, sd = jnp.cos(angd), jnp.sin(angd)
    half = ROPE_DIM // 2
    rest = DIFF_HEAD_DIM - ROPE_DIM
    one = jnp.ones((S, rest), F32)
    zero_r = jnp.zeros((S, rest), F32)
    zero_h = jnp.zeros((S, half), F32)
    da = jnp.concatenate([cd, cd, one], axis=-1)
    dm = jnp.concatenate([-sd, zero_h, zero_r], axis=-1)
    dp = jnp.concatenate([zero_h, sd, zero_r], axis=-1)
    rep = LANES // DIFF_HEAD_DIM
    return rc, rm, rp, jnp.tile(da, (1, rep)), jnp.tile(dm, (1, rep)), jnp.tile(dp, (1, rep))


def _decay_tables():
    C = RET_CHUNK
    H = RET_HEADS
    scale = RET_HEAD_DIM ** -0.5
    log_g = jnp.log1p(-jnp.exp2(-5.0 - jnp.arange(H, dtype=F32)))
    idx = jnp.arange(C, dtype=F32)
    rel = idx[:, None] - idx[None, :]
    intra = jnp.where(rel[None] >= 0,
                      jnp.exp(log_g[:, None, None] * jnp.maximum(rel, 0.0)[None]), 0.0) * scale
    k_decay = jnp.exp(log_g[:, None] * (C - 1.0 - idx)[None, :]) * scale
    q_decay = jnp.exp(log_g[:, None] * (idx + 1.0)[None, :])
    kdec = jnp.broadcast_to(k_decay[:, :, None], (H, C, RET_HEAD_DIM))
    qdec = jnp.broadcast_to(q_decay[:, :, None], (H, C, RET_HEAD_DIM))
    return intra, kdec, qdec


def _gate_weights(wa, wx):
    per = MXU_DIM // LRU_BLOCK
    nt = LRU_BLOCKS // per

    def dense(w):
        w4 = w.reshape(nt, per, LRU_BLOCK, LRU_BLOCK)
        eye = jnp.eye(per, dtype=w.dtype)
        d = jnp.einsum('tpde,pq->tpdqe', w4, eye)
        return d.reshape(nt, MXU_DIM, MXU_DIM)

    return jnp.concatenate([dense(wa), dense(wx)], axis=-1).astype(BF16)


def _cast_kernel(w_ref, o_ref):
    o_ref[...] = w_ref[...].astype(BF16)


def _cast_layer_weight(w, l, br=256, bc=4096):
    _, R, C = w.shape
    br, bc = min(br, R), min(bc, C)
    assert R % br == 0 and C % bc == 0
    return pl.pallas_call(
        _cast_kernel,
        grid=(R // br, C // bc),
        in_specs=[pl.BlockSpec((None, br, bc), lambda i, j: (l, i, j))],
        out_specs=pl.BlockSpec((br, bc), lambda i, j: (i, j)),
        out_shape=jax.ShapeDtypeStruct((R, C), BF16),
        compiler_params=pltpu.CompilerParams(
            dimension_semantics=("parallel", "parallel"),
            vmem_limit_bytes=_vmem_limit(2 * br * bc * (4 + 2) + br * bc * 4)),
        name="cast_weight",
    )(w)


def kernel(x, pre_norm, post_norm, w_in, conv_w, conv_b, lru_wa, lru_ba, lru_wx, lru_bx,
           lru_lambda, diff_lambda, diff_subln, w_branch_a, w_branch_b, w_branch_c, w_out):
    B, S, D = x.shape
    depth = w_in.shape[0]
    T = B * S
    tabs = _rope_tables(S)
    intra, kdec, qdec = _decay_tables()
    x2 = x.reshape(T, D)
    for l in range(depth):
        lam_init = 0.8 - 0.6 * math.exp(-0.3 * l)
        proj = _in_proj(x2, pre_norm[l][None, :], _cast_layer_weight(w_in, l), tabs, S)
        ya = _lru(proj, conv_w[l], conv_b[l][None, :], _gate_weights(lru_wa[l], lru_wx[l]),
                  lru_ba[l][None, :], lru_bx[l][None, :], lru_lambda[l][None, :], B, S)
        yb = _retention(proj, intra, kdec, qdec, B, S)
        yc = _diff_attn(proj, diff_lambda[l], diff_subln[l][:, None], lam_init, B, S)
        m = _merge(ya, yb, yc, w_branch_a[l].astype(BF16), w_branch_b[l].astype(BF16),
                   w_branch_c[l].astype(BF16), proj)
        x2 = _out_proj(m, w_out[l].astype(BF16), x2, post_norm[l][None, :])
    return x2.reshape(B, S, D)
```

```python
import functools
import math

import jax
import jax.numpy as jnp
from jax import lax
from jax.experimental import pallas as pl
from jax.experimental.pallas import tpu as pltpu

F32 = jnp.float32
BF16 = jnp.bfloat16

D_MODEL = 2048
LRU_WIDTH = 1024
LRU_BLOCKS = 16
LRU_BLOCK = LRU_WIDTH // LRU_BLOCKS
CONV_WIDTH = 4
LRU_C = 8.0
RET_HEADS = 8
RET_HEAD_DIM = 128
RET_WIDTH = RET_HEADS * RET_HEAD_DIM
RET_CHUNK = 128
RET_ROT_BASE = 10000.0
DIFF_HEADS = 8
DIFF_HEAD_DIM = 64
DIFF_V_DIM = 2 * DIFF_HEAD_DIM
DIFF_WIDTH = DIFF_HEADS * DIFF_V_DIM
ROPE_THETA = 500000.0
ROPE_DIM = DIFF_HEAD_DIM // 4
N_BRANCH = 3
EPS = 1e-6

LANES = 128
SUBLANES = 8
MXU_DIM = 256
VMEM_BYTES_V7X = 64 * 1024 * 1024
VMEM_RESERVE = 8 * 1024 * 1024
VMEM_MIN_REQUEST = 16 * 1024 * 1024

SEG = 1024
NORM_ROWS = 256
OUT_ROWS = 256
PROJ_ROWS = 256
LOOP_BLOCKS = 4
SEG_XA, SEG_GA, SEG_QR, SEG_KR, SEG_VR, SEG_GR, SEG_QD, SEG_KD, SEG_VD, SEG_GD, SEG_GM = range(11)
SEG_KINDS = ("plain", "silu", "ret", "ret", "plain", "silu", "dif", "dif", "plain", "silu") + (
    "sigmoid",) * (N_BRANCH * D_MODEL // SEG)


def _sigmoid(x):
    return 1.0 / (1.0 + jnp.exp(-x))


def _vmem_limit(nbytes):
    return int(min(max(nbytes, VMEM_MIN_REQUEST), VMEM_BYTES_V7X - VMEM_RESERVE))


def _in_proj_kernel(x_ref, g_ref, w_ref, rc_ref, rm_ref, rp_ref, da_ref, dm_ref, dp_ref, o_ref, h_ref):
    j = pl.program_id(1)

    def normalise():
        for r in range(x_ref.shape[0] // NORM_ROWS):
            rows = slice(r * NORM_ROWS, (r + 1) * NORM_ROWS)
            xf = x_ref[rows, :]
            ms = jnp.mean(xf * xf, axis=-1, keepdims=True)
            h_ref[rows, :] = ((xf * lax.rsqrt(ms + EPS)) * g_ref[...]).astype(BF16)

    def project(epilogues):
        bm, bn = o_ref.shape
        rsub = min(bm, PROJ_ROWS)
        for c in range(bn // MXU_DIM):
            cols = slice(c * MXU_DIM, (c + 1) * MXU_DIM)
            epilogue = epilogues[c * MXU_DIM // SEG]
            for r in range(bm // rsub):
                rows = slice(r * rsub, (r + 1) * rsub)
                acc = jnp.dot(h_ref[rows, :], w_ref[:, cols], preferred_element_type=F32)
                o_ref[rows, cols] = epilogue(acc, rows).astype(BF16)

    def per_lane_chunk(fn):
        def epilogue(acc, rows):
            return jnp.concatenate(
                [fn(acc[:, c * LANES:(c + 1) * LANES], rows) for c in range(MXU_DIM // LANES)],
                axis=1)
        return epilogue

    half = ROPE_DIM // 2
    epilogue_of = {
        "plain": lambda acc, rows: acc,
        "silu": lambda acc, rows: acc * _sigmoid(acc),
        "sigmoid": lambda acc, rows: _sigmoid(acc),
        "ret": per_lane_chunk(
            lambda xc, rows: xc * rc_ref[rows, :] + pltpu.roll(xc, LANES - 1, 1) * rm_ref[rows, :]
            + pltpu.roll(xc, 1, 1) * rp_ref[rows, :]),
        "dif": per_lane_chunk(
            lambda xc, rows: xc * da_ref[rows, :] + pltpu.roll(xc, LANES - half, 1) * dm_ref[rows, :]
            + pltpu.roll(xc, half, 1) * dp_ref[rows, :]),
    }
    nseg = o_ref.shape[1] // SEG

    @pl.when(j == 0)
    def _():
        normalise()

    blocks = {}
    for jv in range(len(SEG_KINDS) // nseg):
        blocks.setdefault(SEG_KINDS[jv * nseg:(jv + 1) * nseg], []).append(jv)
    for kinds, jvs in blocks.items():
        cond = functools.reduce(jnp.logical_or, [j == jv for jv in jvs])

        @pl.when(cond)
        def _(kinds=kinds):
            project([epilogue_of[kind] for kind in kinds])


def _in_proj(x2, g, w, tabs, S, bm=1024, bn=2 * SEG):
    T, D = x2.shape
    N = w.shape[1]
    assert bn % SEG == 0 and T % bm == 0 and S % bm == 0 and N % bn == 0
    nb = S // bm
    tab_spec = pl.BlockSpec((bm, LANES), lambda i, j: (i % nb, 0))
    est = (2 * bm * D * 4 + bm * D * 2 + 2 * D * bn * 2 + 2 * bm * bn * 2
           + 12 * bm * LANES * 4 + 8 * bm * MXU_DIM * 4 + 4 * NORM_ROWS * D * 4)
    return pl.pallas_call(
        _in_proj_kernel,
        grid=(T // bm, N // bn),
        in_specs=[
            pl.BlockSpec((bm, D), lambda i, j: (i, 0)),
            pl.BlockSpec((1, D), lambda i, j: (0, 0)),
            pl.BlockSpec((D, bn), lambda i, j: (0, j)),
            tab_spec, tab_spec, tab_spec, tab_spec, tab_spec, tab_spec,
        ],
        out_specs=pl.BlockSpec((bm, bn), lambda i, j: (i, j)),
        out_shape=jax.ShapeDtypeStruct((T, N), BF16),
        scratch_shapes=[pltpu.VMEM((bm, D), BF16)],
        compiler_params=pltpu.CompilerParams(
            dimension_semantics=("parallel", "arbitrary"),
            vmem_limit_bytes=_vmem_limit(est)),
        name="in_proj",
    )(x2, g, w, *tabs)


def _lru_kernel(xa_ref, ga_ref, cw_ref, cb_ref, wg_ref, ba_ref, bx_ref, lam_ref, o_ref,
                xext, a_s, u_s, hc, alongside=None):
    nb, tc, _ = xa_ref.shape
    pad = SUBLANES

    z = -lam_ref[...]
    sp = jnp.maximum(z, 0.0) + jnp.log1p(jnp.exp(-jnp.abs(z)))
    gw = 2 * MXU_DIM
    for bi in range(nb):
        if alongside is not None:
            alongside(bi)
        xext[bi, pad:pad + tc, :] = xa_ref[bi].astype(F32)
        xc = cb_ref[...] + cw_ref[CONV_WIDTH - 1:CONV_WIDTH, :] * xext[bi, pad:pad + tc, :]
        for k in range(CONV_WIDTH - 1):
            off = pad - (CONV_WIDTH - 1) + k
            xc = xc + cw_ref[k:k + 1, :] * xext[bi, off:off + tc, :]
        xext[bi, 0:pad, :] = xext[bi, tc:tc + pad, :]

        xb = xc.astype(BF16)
        for c in range(LRU_WIDTH // MXU_DIM):
            sl = slice(c * MXU_DIM, (c + 1) * MXU_DIM)
            g = jnp.dot(xb[:, sl], wg_ref[c], preferred_element_type=F32)
            r = _sigmoid(g[:, 0:MXU_DIM] + ba_ref[:, sl])
            ig = _sigmoid(g[:, MXU_DIM:gw] + bx_ref[:, sl])
            log_a = (-LRU_C * r) * sp[:, sl]
            a = jnp.exp(log_a)
            a_s[bi, :, sl] = a
            u_s[bi, :, sl] = jnp.sqrt(-jnp.tanh(log_a) * (a * a + 1.0)) * (ig * xc[:, sl])

    def step(t, hs):
        out = []
        for bi in range(nb):
            h = a_s[bi, pl.ds(t, 1), :] * hs[bi] + u_s[bi, pl.ds(t, 1), :]
            u_s[bi, pl.ds(t, 1), :] = h
            out.append(h)
        return tuple(out)

    hs = lax.fori_loop(0, tc, step, tuple(hc[bi] for bi in range(nb)), unroll=8)
    for bi in range(nb):
        hc[bi] = hs[bi]
        o_ref[bi] = (u_s[bi] * ga_ref[bi].astype(F32)).astype(BF16)


def _lru_ret_kernel(xa_ref, ga_ref, cw_ref, cb_ref, wg_ref, ba_ref, bx_ref, lam_ref,
                    q_ref, k_ref, v_ref, g_ref, intra_ref, kdec_ref, qdec_ref,
                    ya_ref, yb_ref, xext, a_s, u_s, hc, st_ref):
    nb = xa_ref.shape[0]

    @pl.when(pl.program_id(1) == 0)
    def _():
        xext[:, 0:SUBLANES, :] = jnp.zeros((nb, SUBLANES, LRU_WIDTH), F32)
        hc[...] = jnp.zeros_like(hc)
        st_ref[...] = jnp.zeros_like(st_ref)

    def retention(bi):
        _ret_kernel(q_ref.at[bi], k_ref.at[bi], v_ref.at[bi], g_ref.at[bi],
                    intra_ref, kdec_ref, qdec_ref, yb_ref.at[bi], st_ref.at[bi])

    _lru_kernel(xa_ref, ga_ref, cw_ref, cb_ref, wg_ref, ba_ref, bx_ref, lam_ref, ya_ref,
                xext, a_s, u_s, hc, alongside=retention)


def _lru_retention(proj, cw, cb, wg, ba, bx, lam, intra, kdec, qdec, B, S, tc=256, nb=4):
    T = proj.shape[0]
    W = LRU_WIDTH
    C = RET_CHUNK
    assert B % nb == 0 and S % tc == 0 and tc % C == 0 and RET_WIDTH == W
    proj3 = proj.reshape(B, S, proj.shape[1])
    vec = pl.BlockSpec((1, W), lambda b, n: (0, 0))

    def col(seg):
        return pl.BlockSpec((nb, tc, W), lambda b, n: (b, n, seg))

    def table(cols):
        return pl.BlockSpec((RET_HEADS, C, cols), lambda b, n: (0, 0, 0))

    est = (2 * 8 * nb * tc * W * 2 + 3 * nb * tc * W * 4 + 8 * tc * W * 4
           + 2 * RET_HEADS * C * (C + 2 * RET_HEAD_DIM) * 4
           + nb * RET_HEADS * RET_HEAD_DIM * RET_HEAD_DIM * 4 + 16 * RET_HEADS * C * C * 4)
    ya, yb = pl.pallas_call(
        _lru_ret_kernel,
        grid=(B // nb, S // tc),
        in_specs=[
            col(SEG_XA), col(SEG_GA),
            pl.BlockSpec((CONV_WIDTH, W), lambda b, n: (0, 0)),
            vec,
            pl.BlockSpec((W // MXU_DIM, MXU_DIM, 2 * MXU_DIM), lambda b, n: (0, 0, 0)),
            vec, vec, vec,
            col(SEG_QR), col(SEG_KR), col(SEG_VR), col(SEG_GR),
            table(C), table(RET_HEAD_DIM), table(RET_HEAD_DIM),
        ],
        out_specs=[pl.BlockSpec((nb, tc, W), lambda b, n: (b, n, 0)),
                   pl.BlockSpec((nb, tc, W), lambda b, n: (b, n, 0))],
        out_shape=[jax.ShapeDtypeStruct((B, S, W), BF16), jax.ShapeDtypeStruct((B, S, W), BF16)],
        scratch_shapes=[
            pltpu.VMEM((nb, tc + SUBLANES, W), F32),
            pltpu.VMEM((nb, tc, W), F32),
            pltpu.VMEM((nb, tc, W), F32),
            pltpu.VMEM((nb, 1, W), F32),
            pltpu.VMEM((nb, RET_HEADS, RET_HEAD_DIM, RET_HEAD_DIM), F32),
        ],
        compiler_params=pltpu.CompilerParams(
            dimension_semantics=("parallel", "arbitrary"),
            vmem_limit_bytes=_vmem_limit(est)),
        name="lru_retention",
    )(proj3, proj3, cw, cb, wg, ba, bx, lam, proj3, proj3, proj3, proj3, intra, kdec, qdec)
    return ya.reshape(T, W), yb.reshape(T, W)


def _ret_kernel(q_ref, k_ref, v_ref, g_ref, intra_ref, kdec_ref, qdec_ref, o_ref, st_ref):
    tc = q_ref.shape[0]
    C = RET_CHUNK
    dh = RET_HEAD_DIM

    nt_dims = (((1,), (1,)), ((), ()))
    for c in range(tc // C):
        rows = slice(c * C, (c + 1) * C)
        for h in range(RET_HEADS):
            sl = slice(h * dh, (h + 1) * dh)
            intra = intra_ref[h]
            kdec = kdec_ref[h]
            qdec = qdec_ref[h]
            cdec = qdec[C - 1:C, :]
            q = q_ref[rows, sl]
            k = k_ref[rows, sl]
            v = v_ref[rows, sl]
            state = st_ref[h]
            scores = lax.dot_general(q, k, nt_dims, preferred_element_type=F32) * intra
            inner = jnp.dot(scores.astype(BF16), v, preferred_element_type=F32)
            qd = (q.astype(F32) * qdec).astype(BF16)
            cross = jnp.dot(qd, state.astype(BF16), preferred_element_type=F32)
            kd = (k.astype(F32) * kdec).astype(BF16)
            kv = lax.dot_general(kd, v, (((0,), (0,)), ((), ())),
                                 preferred_element_type=F32)
            st_ref[h] = state * cdec + kv
            out = inner + cross
            mu = jnp.mean(out, axis=-1, keepdims=True)
            d = out - mu
            var = jnp.mean(d * d, axis=-1, keepdims=True)
            y = d * lax.rsqrt(var + EPS)
            o_ref[rows, sl] = (y * g_ref[rows, sl].astype(F32)).astype(BF16)


def _diff_kernel(q_ref, k_ref, v_ref, g_ref, dl_ref, sub_ref, o_ref, vxt_ref, m_ref, a_ref, s_ref,
                 tri_ref, qq_ref, *, lam_init, cs):
    qi = pl.program_id(2)
    bq = q_ref.shape[0]
    nkb, vx_rows, bk = vxt_ref.shape
    d = DIFF_HEAD_DIM
    dv = 2 * d
    nt_dims = (((1,), (1,)), ((), ()))

    @pl.when(qi == 0)
    def _():
        for c in range(nkb):
            vxt_ref[c, 0:dv, :] = v_ref[c * bk:(c + 1) * bk, :].astype(F32).T.astype(BF16)
            vxt_ref[c, dv:vx_rows, :] = jnp.ones((vx_rows - dv, bk), BF16)

    def prep_queries(c):
        lane = lax.broadcasted_iota(jnp.int32, (cs, dv), 1)
        q0 = (c * cs) % bq
        qs = q_ref[q0:q0 + cs, :].astype(F32) * (d ** -0.5 * math.log2(math.e))
        keep = (lane < d) if c * cs < bq else (lane >= d)
        qq_ref[c * cs:(c + 1) * cs, :] = jnp.where(keep, qs, 0.0).astype(BF16)

    ncol = 2 * bq // cs

    tri_ref[...] = jnp.where(lax.broadcasted_iota(jnp.int32, (cs, cs), 0)
                             <= lax.broadcasted_iota(jnp.int32, (cs, cs), 1), 0.0, -jnp.inf)

    def key_plan(c, kv0):
        nsubk = bk // cs
        if kv0 is None:
            return nsubk, False
        delta = (c * cs) % bq - kv0
        if delta < 0:
            return 0, False
        nfullk = min(nsubk, delta // cs)
        return nfullk, nfullk < nsubk

    def scores(t, slot, kv0=None, first=False):
        k = k_ref[pl.ds(pl.multiple_of(t * bk, bk), bk), :]
        for c in range(ncol):
            if first:
                prep_queries(c)
            nfullk, tri = key_plan(c, kv0)
            used = (nfullk + tri) * cs
            if used:
                s_ref[slot, 0:used, c * cs:(c + 1) * cs] = lax.dot_general(
                    k[0:used], qq_ref[c * cs:(c + 1) * cs, :], nt_dims,
                    preferred_element_type=F32)

    def softmax_pv(t, slot, kv0):
        vxt = vxt_ref[t]
        for c in range(ncol):
            nfullk, tri = key_plan(c, kv0)
            used = (nfullk + tri) * cs
            if not used:
                continue
            cols = slice(c * cs, (c + 1) * cs)
            parts = []
            if nfullk:
                parts.append(s_ref[slot, 0:nfullk * cs, cols])
            if tri:
                parts.append(s_ref[slot, nfullk * cs:used, cols] + tri_ref[...])
            m_new = m_prev = m_ref[:, cols]
            for s in parts:
                m_new = jnp.maximum(m_new, jnp.max(s, axis=0, keepdims=True))
            alpha = jnp.exp2(m_prev - m_new)
            p = [jnp.exp2(s - m_new).astype(BF16) for s in parts]
            p = p[0] if len(p) == 1 else jnp.concatenate(p, axis=0)
            pv = jnp.dot(vxt[:, 0:used], p, preferred_element_type=F32)
            a_ref[:, cols] = alpha * a_ref[:, cols] + pv
            m_ref[:, cols] = m_new

    ndiag = bq // bk
    nfull = qi * ndiag

    def body(u, carry):
        for i in range(LOOP_BLOCKS):
            t = LOOP_BLOCKS * u + i
            scores(t + 1, (i + 1) % 2)
            softmax_pv(t, i % 2, None)
        return carry

    scores(0, 0, first=True)
    m_ref[...] = jnp.full(m_ref.shape, -1e30, F32)
    a_ref[...] = jnp.zeros(a_ref.shape, F32)
    lax.fori_loop(0, nfull // LOOP_BLOCKS, body, 0)
    for jj in range(ndiag):
        if jj + 1 < ndiag:
            scores(nfull + jj + 1, (jj + 1) % 2, (jj + 1) * bk)
        softmax_pv(nfull + jj, jj % 2, jj * bk)

    dl = dl_ref[...].astype(F32)
    lam = (jnp.exp(jnp.sum(dl[0:1, :] * dl[1:2, :], axis=-1, keepdims=True))
           - jnp.exp(jnp.sum(dl[2:3, :] * dl[3:4, :], axis=-1, keepdims=True)) + lam_init)
    o1 = a_ref[0:dv, 0:bq] / a_ref[dv:dv + 1, 0:bq]
    o2 = a_ref[0:dv, bq:2 * bq] / a_ref[dv:dv + 1, bq:2 * bq]
    of = o1 - lam * o2
    of = of * lax.rsqrt(jnp.mean(of * of, axis=0, keepdims=True) + EPS) * sub_ref[...]
    o = (of * (1.0 - lam_init)).T
    o_ref[...] = (o * g_ref[...].astype(F32)).astype(BF16)


def _diff_attn(proj, dl, sub, lam_init, B, S, bq=2048, bk=512, cs=256):
    T = proj.shape[0]
    nq = S // bq
    H = DIFF_HEADS
    dv = DIFF_V_DIM
    per_seg = SEG // dv
    ones_rows = 2 * SUBLANES
    assert bq % (LOOP_BLOCKS * bk) == 0 and LOOP_BLOCKS % 2 == 0 and S % bq == 0 and bk % cs == 0

    def qcol(seg):
        return pl.BlockSpec((bq, dv), lambda b, h, i: (b * nq + i, seg * per_seg + h))

    def kvcol(seg):
        return pl.BlockSpec((S, dv), lambda b, h, i: (b, seg * per_seg + h))

    est = (2 * 2 * S * dv * 2 + S * (dv + ones_rows) * 2 + 2 * 3 * bq * dv * 2
           + (dv + ones_rows + 1) * 2 * bq * 4 + 2 * bk * 2 * bq * 4 + cs * cs * 4 + 8 * LOOP_BLOCKS * bk * cs * 4
           + 6 * bq * dv * 4)
    return pl.pallas_call(
        functools.partial(_diff_kernel, lam_init=lam_init, cs=cs),
        grid=(B, H, nq),
        in_specs=[
            qcol(SEG_QD), kvcol(SEG_KD), kvcol(SEG_VD), qcol(SEG_GD),
            pl.BlockSpec((4, DIFF_HEAD_DIM), lambda b, h, i: (0, 0)),
            pl.BlockSpec((dv, 1), lambda b, h, i: (0, 0)),
        ],
        out_specs=pl.BlockSpec((bq, dv), lambda b, h, i: (b * nq + i, h)),
        out_shape=jax.ShapeDtypeStruct((T, DIFF_WIDTH), BF16),
        scratch_shapes=[
            pltpu.VMEM((S // bk, dv + ones_rows, bk), BF16),
            pltpu.VMEM((1, 2 * bq), F32),
            pltpu.VMEM((dv + ones_rows, 2 * bq), F32),
            pltpu.VMEM((2, bk, 2 * bq), F32),
            pltpu.VMEM((cs, cs), F32),
            pltpu.VMEM((2 * bq, dv), BF16),
        ],
        compiler_params=pltpu.CompilerParams(
            dimension_semantics=("parallel", "parallel", "arbitrary"),
            vmem_limit_bytes=_vmem_limit(est)),
        name="diff_attn",
    )(proj, proj, proj, proj, dl, sub)


def _merge_kernel(ya_ref, yb_ref, yc_ref, wa_ref, wb_ref, wc_ref, g0_ref, g1_ref, g2_ref, o_ref):
    for c in range(o_ref.shape[1] // MXU_DIM):
        cols = slice(c * MXU_DIM, (c + 1) * MXU_DIM)
        m = g0_ref[:, cols].astype(F32) * jnp.dot(ya_ref[...], wa_ref[:, cols],
                                                  preferred_element_type=F32)
        m = m + g1_ref[:, cols].astype(F32) * jnp.dot(yb_ref[...], wb_ref[:, cols],
                                                      preferred_element_type=F32)
        m = m + g2_ref[:, cols].astype(F32) * jnp.dot(yc_ref[...], wc_ref[:, cols],
                                                      preferred_element_type=F32)
        o_ref[:, cols] = m.astype(BF16)


def _merge(ya, yb, yc, wa, wb, wc, proj, bm=512):
    T, W = ya.shape
    D = wa.shape[1]
    ybk = pl.BlockSpec((bm, W), lambda i: (i, 0))
    wbk = pl.BlockSpec((W, D), lambda i: (0, 0), pipeline_mode=pl.Buffered(1))
    gm0 = SEG_GM * SEG // D

    def gate(br):
        return pl.BlockSpec((bm, D), lambda i: (i, gm0 + br))

    est = 2 * 3 * bm * W * 2 + 3 * W * D * 2 + 2 * 4 * bm * D * 2 + 8 * bm * MXU_DIM * 4
    return pl.pallas_call(
        _merge_kernel,
        grid=(T // bm,),
        in_specs=[ybk, ybk, ybk, wbk, wbk, wbk, gate(0), gate(1), gate(2)],
        out_specs=pl.BlockSpec((bm, D), lambda i: (i, 0)),
        out_shape=jax.ShapeDtypeStruct((T, D), BF16),
        compiler_params=pltpu.CompilerParams(
            dimension_semantics=("parallel",),
            vmem_limit_bytes=_vmem_limit(est)),
        name="merge",
    )(ya, yb, yc, wa, wb, wc, proj, proj, proj)


def _out_kernel(m_ref, w_ref, x_ref, g_ref, o_ref):
    for r in range(o_ref.shape[0] // OUT_ROWS):
        rows = slice(r * OUT_ROWS, (r + 1) * OUT_ROWS)
        o = jnp.dot(m_ref[rows, :], w_ref[...], preferred_element_type=F32)
        y = o * lax.rsqrt(jnp.mean(o * o, axis=-1, keepdims=True) + EPS)
        o_ref[rows, :] = x_ref[rows, :] + y * g_ref[...]


def _out_proj(m, w, x2, g, bm=512):
    T, D = x2.shape
    est = 2 * bm * D * 2 + D * D * 2 + 4 * bm * D * 4 + 6 * OUT_ROWS * D * 4
    return pl.pallas_call(
        _out_kernel,
        grid=(T // bm,),
        in_specs=[
            pl.BlockSpec((bm, D), lambda i: (i, 0)),
            pl.BlockSpec((D, D), lambda i: (0, 0), pipeline_mode=pl.Buffered(1)),
            pl.BlockSpec((bm, D), lambda i: (i, 0)),
            pl.BlockSpec((1, D), lambda i: (0, 0)),
        ],
        out_specs=pl.BlockSpec((bm, D), lambda i: (i, 0)),
        out_shape=jax.ShapeDtypeStruct((T, D), F32),
        compiler_params=pltpu.CompilerParams(
            dimension_semantics=("parallel",),
            vmem_limit_bytes=_vmem_limit(est)),
        name="out_proj",
    )(m, w, x2, g)


def _rope_tables(S):
    pos = jnp.arange(S, dtype=F32)
    ret_freq = 1.0 / (RET_ROT_BASE ** jnp.linspace(0.0, 1.0, RET_HEAD_DIM // 2, dtype=F32))
    ang = pos[:, None] * ret_freq[None, :]
    c, s = jnp.cos(ang), jnp.sin(ang)
    zero = jnp.zeros_like(s)
    rc = jnp.stack([c, c], axis=-1).reshape(S, RET_HEAD_DIM)
    rm = jnp.stack([-s, zero], axis=-1).reshape(S, RET_HEAD_DIM)
    rp = jnp.stack([zero, s], axis=-1).reshape(S, RET_HEAD_DIM)
    inv_freq = ROPE_THETA ** (-jnp.arange(0, ROPE_DIM, 2, dtype=F32) / ROPE_DIM)
    angd = pos[:, None] * inv_freq[None, :]
    cd, sd = jnp.cos(angd), jnp.sin(angd)
    half = ROPE_DIM // 2
    rest = DIFF_HEAD_DIM - ROPE_DIM
    one = jnp.ones((S, rest), F32)
    zero_r = jnp.zeros((S, rest), F32)
    zero_h = jnp.zeros((S, half), F32)
    da = jnp.concatenate([cd, cd, one], axis=-1)
    dm = jnp.concatenate([-sd, zero_h, zero_r], axis=-1)
    dp = jnp.concatenate([zero_h, sd, zero_r], axis=-1)
    rep = LANES // DIFF_HEAD_DIM
    return rc, rm, rp, jnp.tile(da, (1, rep)), jnp.tile(dm, (1, rep)), jnp.tile(dp, (1, rep))


def _decay_tables():
    C = RET_CHUNK
    H = RET_HEADS
    scale = RET_HEAD_DIM ** -0.5
    log_g = jnp.log1p(-jnp.exp2(-5.0 - jnp.arange(H, dtype=F32)))
    idx = jnp.arange(C, dtype=F32)
    rel = idx[:, None] - idx[None, :]
    intra = jnp.where(rel[None] >= 0,
                      jnp.exp(log_g[:, None, None] * jnp.maximum(rel, 0.0)[None]), 0.0) * scale
    k_decay = jnp.exp(log_g[:, None] * (C - 1.0 - idx)[None, :]) * scale
    q_decay = jnp.exp(log_g[:, None] * (idx + 1.0)[None, :])
    kdec = jnp.broadcast_to(k_decay[:, :, None], (H, C, RET_HEAD_DIM))
    qdec = jnp.broadcast_to(q_decay[:, :, None], (H, C, RET_HEAD_DIM))
    return intra, kdec, qdec


def _gate_weights(wa, wx):
    per = MXU_DIM // LRU_BLOCK
    nt = LRU_BLOCKS // per

    def dense(w):
        w4 = w.reshape(nt, per, LRU_BLOCK, LRU_BLOCK)
        eye = jnp.eye(per, dtype=w.dtype)
        d = jnp.einsum('tpde,pq->tpdqe', w4, eye)
        return d.reshape(nt, MXU_DIM, MXU_DIM)

    return jnp.concatenate([dense(wa), dense(wx)], axis=-1).astype(BF16)


def _cast_kernel(w_ref, o_ref):
    o_ref[...] = w_ref[...].astype(BF16)


def _cast_layer_weight(w, l, br=256, bc=4096):
    _, R, C = w.shape
    br, bc = min(br, R), min(bc, C)
    assert R % br == 0 and C % bc == 0
    return pl.pallas_call(
        _cast_kernel,
        grid=(R // br, C // bc),
        in_specs=[pl.BlockSpec((None, br, bc), lambda i, j: (l, i, j))],
        out_specs=pl.BlockSpec((br, bc), lambda i, j: (i, j)),
        out_shape=jax.ShapeDtypeStruct((R, C), BF16),
        compiler_params=pltpu.CompilerParams(
            dimension_semantics=("parallel", "parallel"),
            vmem_limit_bytes=_vmem_limit(2 * br * bc * (4 + 2) + br * bc * 4)),
        name="cast_weight",
    )(w)


def kernel(x, pre_norm, post_norm, w_in, conv_w, conv_b, lru_wa, lru_ba, lru_wx, lru_bx,
           lru_lambda, diff_lambda, diff_subln, w_branch_a, w_branch_b, w_branch_c, w_out):
    B, S, D = x.shape
    depth = w_in.shape[0]
    T = B * S
    tabs = _rope_tables(S)
    intra, kdec, qdec = _decay_tables()
    x2 = x.reshape(T, D)
    for l in range(depth):
        lam_init = 0.8 - 0.6 * math.exp(-0.3 * l)
        proj = _in_proj(x2, pre_norm[l][None, :], _cast_layer_weight(w_in, l), tabs, S)
        ya, yb = _lru_retention(
            proj, conv_w[l], conv_b[l][None, :], _gate_weights(lru_wa[l], lru_wx[l]),
            lru_ba[l][None, :], lru_bx[l][None, :], lru_lambda[l][None, :], intra, kdec, qdec, B, S)
        yc = _diff_attn(proj, diff_lambda[l], diff_subln[l][:, None], lam_init, B, S)
        m = _merge(ya, yb, yc, w_branch_a[l].astype(BF16), w_branch_b[l].astype(BF16),
                   w_branch_c[l].astype(BF16), proj)
        x2 = _out_proj(m, w_out[l].astype(BF16), x2, post_norm[l][None, :])
    return x2.reshape(B, S, D)
```

```python
import functools
import math

import jax
import jax.numpy as jnp
from jax import lax
from jax.experimental import pallas as pl
from jax.experimental.pallas import tpu as pltpu

F32 = jnp.float32
BF16 = jnp.bfloat16

D_MODEL = 2048
LRU_WIDTH = 1024
LRU_BLOCKS = 16
LRU_BLOCK = LRU_WIDTH // LRU_BLOCKS
CONV_WIDTH = 4
LRU_C = 8.0
RET_HEADS = 8
RET_HEAD_DIM = 128
RET_WIDTH = RET_HEADS * RET_HEAD_DIM
RET_CHUNK = 128
RET_ROT_BASE = 10000.0
DIFF_HEADS = 8
DIFF_HEAD_DIM = 64
DIFF_V_DIM = 2 * DIFF_HEAD_DIM
DIFF_WIDTH = DIFF_HEADS * DIFF_V_DIM
ROPE_THETA = 500000.0
ROPE_DIM = DIFF_HEAD_DIM // 4
N_BRANCH = 3
EPS = 1e-6

LANES = 128
SUBLANES = 8
MXU_DIM = 256
VMEM_BYTES_V7X = 64 * 1024 * 1024
VMEM_RESERVE = 8 * 1024 * 1024
VMEM_MIN_REQUEST = 16 * 1024 * 1024

SEG = 1024
NORM_ROWS = 256
OUT_ROWS = 256
PROJ_ROWS = 256
LOOP_BLOCKS = 4
SEG_XA, SEG_GA, SEG_QR, SEG_KR, SEG_VR, SEG_GR, SEG_QD, SEG_KD, SEG_VD, SEG_GD, SEG_GM = range(11)
SEG_KINDS = ("plain", "silu", "ret", "ret", "plain", "silu", "dif", "dif", "plain", "silu") + (
    "sigmoid",) * (N_BRANCH * D_MODEL // SEG)


def _sigmoid(x):
    return 1.0 / (1.0 + jnp.exp(-x))


def _vmem_limit(nbytes):
    return int(min(max(nbytes, VMEM_MIN_REQUEST), VMEM_BYTES_V7X - VMEM_RESERVE))


def _in_proj_kernel(x_ref, g_ref, w_ref, rc_ref, rm_ref, rp_ref, da_ref, dm_ref, dp_ref, o_ref, h_ref):
    j = pl.program_id(1)

    def normalise():
        for r in range(x_ref.shape[0] // NORM_ROWS):
            rows = slice(r * NORM_ROWS, (r + 1) * NORM_ROWS)
            xf = x_ref[rows, :]
            ms = jnp.mean(xf * xf, axis=-1, keepdims=True)
            h_ref[rows, :] = ((xf * lax.rsqrt(ms + EPS)) * g_ref[...]).astype(BF16)

    def project(epilogues):
        bm, bn = o_ref.shape
        rsub = min(bm, PROJ_ROWS)
        for c in range(bn // MXU_DIM):
            cols = slice(c * MXU_DIM, (c + 1) * MXU_DIM)
            epilogue = epilogues[c * MXU_DIM // SEG]
            for r in range(bm // rsub):
                rows = slice(r * rsub, (r + 1) * rsub)
                acc = jnp.dot(h_ref[rows, :], w_ref[:, cols], preferred_element_type=F32)
                o_ref[rows, cols] = epilogue(acc, rows).astype(BF16)

    def per_lane_chunk(fn):
        def epilogue(acc, rows):
            return jnp.concatenate(
                [fn(acc[:, c * LANES:(c + 1) * LANES], rows) for c in range(MXU_DIM // LANES)],
                axis=1)
        return epilogue

    half = ROPE_DIM // 2
    epilogue_of = {
        "plain": lambda acc, rows: acc,
        "silu": lambda acc, rows: acc * _sigmoid(acc),
        "sigmoid": lambda acc, rows: _sigmoid(acc),
        "ret": per_lane_chunk(
            lambda xc, rows: xc * rc_ref[rows, :] + pltpu.roll(xc, LANES - 1, 1) * rm_ref[rows, :]
            + pltpu.roll(xc, 1, 1) * rp_ref[rows, :]),
        "dif": per_lane_chunk(
            lambda xc, rows: xc * da_ref[rows, :] + pltpu.roll(xc, LANES - half, 1) * dm_ref[rows, :]
            + pltpu.roll(xc, half, 1) * dp_ref[rows, :]),
    }
    nseg = o_ref.shape[1] // SEG

    @pl.when(j == 0)
    def _():
        normalise()

    blocks = {}
    for jv in range(len(SEG_KINDS) // nseg):
        blocks.setdefault(SEG_KINDS[jv * nseg:(jv + 1) * nseg], []).append(jv)
    for kinds, jvs in blocks.items():
        cond = functools.reduce(jnp.logical_or, [j == jv for jv in jvs])

        @pl.when(cond)
        def _(kinds=kinds):
            project([epilogue_of[kind] for kind in kinds])


def _in_proj(x2, g, w, tabs, S, bm=1024, bn=2 * SEG):
    T, D = x2.shape
    N = w.shape[1]
    assert bn % SEG == 0 and T % bm == 0 and S % bm == 0 and N % bn == 0
    nb = S // bm
    tab_spec = pl.BlockSpec((bm, LANES), lambda i, j: (i % nb, 0))
    est = (2 * bm * D * 4 + bm * D * 2 + 2 * D * bn * 2 + 2 * bm * bn * 2
           + 12 * bm * LANES * 4 + 8 * bm * MXU_DIM * 4 + 4 * NORM_ROWS * D * 4)
    return pl.pallas_call(
        _in_proj_kernel,
        grid=(T // bm, N // bn),
        in_specs=[
            pl.BlockSpec((bm, D), lambda i, j: (i, 0)),
            pl.BlockSpec((1, D), lambda i, j: (0, 0)),
            pl.BlockSpec((D, bn), lambda i, j: (0, j)),
            tab_spec, tab_spec, tab_spec, tab_spec, tab_spec, tab_spec,
        ],
        out_specs=pl.BlockSpec((bm, bn), lambda i, j: (i, j)),
        out_shape=jax.ShapeDtypeStruct((T, N), BF16),
        scratch_shapes=[pltpu.VMEM((bm, D), BF16)],
        compiler_params=pltpu.CompilerParams(
            dimension_semantics=("parallel", "arbitrary"),
            vmem_limit_bytes=_vmem_limit(est)),
        name="in_proj",
    )(x2, g, w, *tabs)


def _lru_kernel(xa_ref, ga_ref, cw_ref, cb_ref, wg_ref, ba_ref, bx_ref, lam_ref, o_ref,
                xext, a_s, u_s, hc, alongside=None):
    nb, tc, _ = xa_ref.shape
    pad = SUBLANES

    z = -lam_ref[...]
    sp = jnp.maximum(z, 0.0) + jnp.log1p(jnp.exp(-jnp.abs(z)))
    gw = 2 * MXU_DIM
    for bi in range(nb):
        xext[bi, pad:pad + tc, :] = xa_ref[bi].astype(F32)
        xc = cb_ref[...] + cw_ref[CONV_WIDTH - 1:CONV_WIDTH, :] * xext[bi, pad:pad + tc, :]
        for k in range(CONV_WIDTH - 1):
            off = pad - (CONV_WIDTH - 1) + k
            xc = xc + cw_ref[k:k + 1, :] * xext[bi, off:off + tc, :]
        xext[bi, 0:pad, :] = xext[bi, tc:tc + pad, :]

        xb = xc.astype(BF16)
        for c in range(LRU_WIDTH // MXU_DIM):
            if alongside is not None:
                alongside(bi, c, LRU_WIDTH // MXU_DIM)
            sl = slice(c * MXU_DIM, (c + 1) * MXU_DIM)
            g = jnp.dot(xb[:, sl], wg_ref[c], preferred_element_type=F32)
            r = _sigmoid(g[:, 0:MXU_DIM] + ba_ref[:, sl])
            ig = _sigmoid(g[:, MXU_DIM:gw] + bx_ref[:, sl])
            log_a = (-LRU_C * r) * sp[:, sl]
            a = jnp.exp(log_a)
            a_s[bi, :, sl] = a
            u_s[bi, :, sl] = jnp.sqrt(-jnp.tanh(log_a) * (a * a + 1.0)) * (ig * xc[:, sl])

    def step(t, hs):
        out = []
        for bi in range(nb):
            h = a_s[bi, pl.ds(t, 1), :] * hs[bi] + u_s[bi, pl.ds(t, 1), :]
            u_s[bi, pl.ds(t, 1), :] = h
            out.append(h)
        return tuple(out)

    hs = lax.fori_loop(0, tc, step, tuple(hc[bi] for bi in range(nb)), unroll=8)
    for bi in range(nb):
        hc[bi] = hs[bi]
        o_ref[bi] = (u_s[bi] * ga_ref[bi].astype(F32)).astype(BF16)


def _lru_ret_kernel(xa_ref, ga_ref, cw_ref, cb_ref, wg_ref, ba_ref, bx_ref, lam_ref,
                    q_ref, k_ref, v_ref, g_ref, intra_ref, kdec_ref, qdec_ref,
                    ya_ref, yb_ref, xext, a_s, u_s, hc, st_ref):
    nb = xa_ref.shape[0]

    @pl.when(pl.program_id(1) == 0)
    def _():
        xext[:, 0:SUBLANES, :] = jnp.zeros((nb, SUBLANES, LRU_WIDTH), F32)
        hc[...] = jnp.zeros_like(hc)
        st_ref[...] = jnp.zeros_like(st_ref)

    def retention(bi, part, nparts):
        _ret_kernel(q_ref.at[bi], k_ref.at[bi], v_ref.at[bi], g_ref.at[bi],
                    intra_ref, kdec_ref, qdec_ref, yb_ref.at[bi], st_ref.at[bi], part, nparts)

    _lru_kernel(xa_ref, ga_ref, cw_ref, cb_ref, wg_ref, ba_ref, bx_ref, lam_ref, ya_ref,
                xext, a_s, u_s, hc, alongside=retention)


def _lru_retention(proj, cw, cb, wg, ba, bx, lam, intra, kdec, qdec, B, S, tc=256, nb=4):
    T = proj.shape[0]
    W = LRU_WIDTH
    C = RET_CHUNK
    assert B % nb == 0 and S % tc == 0 and tc % C == 0 and RET_WIDTH == W
    proj3 = proj.reshape(B, S, proj.shape[1])
    vec = pl.BlockSpec((1, W), lambda b, n: (0, 0))

    def col(seg):
        return pl.BlockSpec((nb, tc, W), lambda b, n: (b, n, seg))

    def table(cols):
        return pl.BlockSpec((RET_HEADS, C, cols), lambda b, n: (0, 0, 0))

    est = (2 * 8 * nb * tc * W * 2 + 3 * nb * tc * W * 4 + 8 * tc * W * 4
           + 2 * RET_HEADS * C * (C + 2 * RET_HEAD_DIM) * 4
           + nb * RET_HEADS * RET_HEAD_DIM * RET_HEAD_DIM * 4 + 16 * RET_HEADS * C * C * 4)
    ya, yb = pl.pallas_call(
        _lru_ret_kernel,
        grid=(B // nb, S // tc),
        in_specs=[
            col(SEG_XA), col(SEG_GA),
            pl.BlockSpec((CONV_WIDTH, W), lambda b, n: (0, 0)),
            vec,
            pl.BlockSpec((W // MXU_DIM, MXU_DIM, 2 * MXU_DIM), lambda b, n: (0, 0, 0)),
            vec, vec, vec,
            col(SEG_QR), col(SEG_KR), col(SEG_VR), col(SEG_GR),
            table(C), table(RET_HEAD_DIM), table(RET_HEAD_DIM),
        ],
        out_specs=[pl.BlockSpec((nb, tc, W), lambda b, n: (b, n, 0)),
                   pl.BlockSpec((nb, tc, W), lambda b, n: (b, n, 0))],
        out_shape=[jax.ShapeDtypeStruct((B, S, W), BF16), jax.ShapeDtypeStruct((B, S, W), BF16)],
        scratch_shapes=[
            pltpu.VMEM((nb, tc + SUBLANES, W), F32),
            pltpu.VMEM((nb, tc, W), F32),
            pltpu.VMEM((nb, tc, W), F32),
            pltpu.VMEM((nb, 1, W), F32),
            pltpu.VMEM((nb, RET_HEADS, RET_HEAD_DIM, RET_HEAD_DIM), F32),
        ],
        compiler_params=pltpu.CompilerParams(
            dimension_semantics=("parallel", "arbitrary"),
            vmem_limit_bytes=_vmem_limit(est)),
        name="lru_retention",
    )(proj3, proj3, cw, cb, wg, ba, bx, lam, proj3, proj3, proj3, proj3, intra, kdec, qdec)
    return ya.reshape(T, W), yb.reshape(T, W)


def _ret_kernel(q_ref, k_ref, v_ref, g_ref, intra_ref, kdec_ref, qdec_ref, o_ref, st_ref,
                part=0, nparts=1):
    tc = q_ref.shape[0]
    C = RET_CHUNK
    dh = RET_HEAD_DIM
    nunits = (tc // C) * RET_HEADS
    assert nunits % nparts == 0
    lo, hi = part * nunits // nparts, (part + 1) * nunits // nparts

    nt_dims = (((1,), (1,)), ((), ()))
    for c in range(tc // C):
        rows = slice(c * C, (c + 1) * C)
        for h in range(RET_HEADS):
            if not lo <= c * RET_HEADS + h < hi:
                continue
            sl = slice(h * dh, (h + 1) * dh)
            intra = intra_ref[h]
            kdec = kdec_ref[h]
            qdec = qdec_ref[h]
            cdec = qdec[C - 1:C, :]
            q = q_ref[rows, sl]
            k = k_ref[rows, sl]
            v = v_ref[rows, sl]
            state = st_ref[h]
            scores = lax.dot_general(q, k, nt_dims, preferred_element_type=F32) * intra
            inner = jnp.dot(scores.astype(BF16), v, preferred_element_type=F32)
            qd = (q.astype(F32) * qdec).astype(BF16)
            cross = jnp.dot(qd, state.astype(BF16), preferred_element_type=F32)
            kd = (k.astype(F32) * kdec).astype(BF16)
            kv = lax.dot_general(kd, v, (((0,), (0,)), ((), ())),
                                 preferred_element_type=F32)
            st_ref[h] = state * cdec + kv
            out = inner + cross
            mu = jnp.mean(out, axis=-1, keepdims=True)
            d = out - mu
            var = jnp.mean(d * d, axis=-1, keepdims=True)
            y = d * lax.rsqrt(var + EPS)
            o_ref[rows, sl] = (y * g_ref[rows, sl].astype(F32)).astype(BF16)


def _diff_kernel(q_ref, k_ref, v_ref, g_ref, dl_ref, sub_ref, o_ref, vxt_ref, m_ref, a_ref, s_ref,
                 tri_ref, qq_ref, *, lam_init, cs):
    qi = pl.program_id(2)
    bq = q_ref.shape[0]
    nkb, vx_rows, bk = vxt_ref.shape
    d = DIFF_HEAD_DIM
    dv = 2 * d
    nt_dims = (((1,), (1,)), ((), ()))

    @pl.when(qi == 0)
    def _():
        for c in range(nkb):
            vxt_ref[c, 0:dv, :] = v_ref[c * bk:(c + 1) * bk, :].astype(F32).T.astype(BF16)
            vxt_ref[c, dv:vx_rows, :] = jnp.ones((vx_rows - dv, bk), BF16)

    def prep_queries(c):
        lane = lax.broadcasted_iota(jnp.int32, (cs, dv), 1)
        q0 = (c * cs) % bq
        qs = q_ref[q0:q0 + cs, :].astype(F32) * (d ** -0.5 * math.log2(math.e))
        keep = (lane < d) if c * cs < bq else (lane >= d)
        qq_ref[c * cs:(c + 1) * cs, :] = jnp.where(keep, qs, 0.0).astype(BF16)

    ncol = 2 * bq // cs

    tri_ref[...] = jnp.where(lax.broadcasted_iota(jnp.int32, (cs, cs), 0)
                             <= lax.broadcasted_iota(jnp.int32, (cs, cs), 1), 0.0, -jnp.inf)

    def key_plan(c, kv0):
        nsubk = bk // cs
        if kv0 is None:
            return nsubk, False
        delta = (c * cs) % bq - kv0
        if delta < 0:
            return 0, False
        nfullk = min(nsubk, delta // cs)
        return nfullk, nfullk < nsubk

    def scores(t, slot, kv0=None, first=False):
        k = k_ref[pl.ds(pl.multiple_of(t * bk, bk), bk), :]
        for c in range(ncol):
            if first:
                prep_queries(c)
            nfullk, tri = key_plan(c, kv0)
            used = (nfullk + tri) * cs
            if used:
                s_ref[slot, 0:used, c * cs:(c + 1) * cs] = lax.dot_general(
                    k[0:used], qq_ref[c * cs:(c + 1) * cs, :], nt_dims,
                    preferred_element_type=F32)

    def softmax_pv(t, slot, kv0):
        vxt = vxt_ref[t]
        for c in range(ncol):
            nfullk, tri = key_plan(c, kv0)
            used = (nfullk + tri) * cs
            if not used:
                continue
            cols = slice(c * cs, (c + 1) * cs)
            parts = []
            if nfullk:
                parts.append(s_ref[slot, 0:nfullk * cs, cols])
            if tri:
                parts.append(s_ref[slot, nfullk * cs:used, cols] + tri_ref[...])
            m_new = m_prev = m_ref[:, cols]
            for s in parts:
                m_new = jnp.maximum(m_new, jnp.max(s, axis=0, keepdims=True))
            alpha = jnp.exp2(m_prev - m_new)
            p = [jnp.exp2(s - m_new).astype(BF16) for s in parts]
            p = p[0] if len(p) == 1 else jnp.concatenate(p, axis=0)
            pv = jnp.dot(vxt[:, 0:used], p, preferred_element_type=F32)
            a_ref[:, cols] = alpha * a_ref[:, cols] + pv
            m_ref[:, cols] = m_new

    ndiag = bq // bk
    nfull = qi * ndiag

    def body(u, carry):
        for i in range(LOOP_BLOCKS):
            t = LOOP_BLOCKS * u + i
            scores(t + 1, (i + 1) % 2)
            softmax_pv(t, i % 2, None)
        return carry

    scores(0, 0, first=True)
    m_ref[...] = jnp.full(m_ref.shape, -1e30, F32)
    a_ref[...] = jnp.zeros(a_ref.shape, F32)
    lax.fori_loop(0, nfull // LOOP_BLOCKS, body, 0)
    for jj in range(ndiag):
        if jj + 1 < ndiag:
            scores(nfull + jj + 1, (jj + 1) % 2, (jj + 1) * bk)
        softmax_pv(nfull + jj, jj % 2, jj * bk)

    dl = dl_ref[...].astype(F32)
    lam = (jnp.exp(jnp.sum(dl[0:1, :] * dl[1:2, :], axis=-1, keepdims=True))
           - jnp.exp(jnp.sum(dl[2:3, :] * dl[3:4, :], axis=-1, keepdims=True)) + lam_init)
    o1 = a_ref[0:dv, 0:bq] / a_ref[dv:dv + 1, 0:bq]
    o2 = a_ref[0:dv, bq:2 * bq] / a_ref[dv:dv + 1, bq:2 * bq]
    of = o1 - lam * o2
    of = of * lax.rsqrt(jnp.mean(of * of, axis=0, keepdims=True) + EPS) * sub_ref[...]
    o = (of * (1.0 - lam_init)).T
    o_ref[...] = (o * g_ref[...].astype(F32)).astype(BF16)


def _diff_attn(proj, dl, sub, lam_init, B, S, bq=2048, bk=512, cs=256):
    T = proj.shape[0]
    nq = S // bq
    H = DIFF_HEADS
    dv = DIFF_V_DIM
    per_seg = SEG // dv
    ones_rows = 2 * SUBLANES
    assert bq % (LOOP_BLOCKS * bk) == 0 and LOOP_BLOCKS % 2 == 0 and S % bq == 0 and bk % cs == 0

    def qcol(seg):
        return pl.BlockSpec((bq, dv), lambda b, h, i: (b * nq + i, seg * per_seg + h))

    def kvcol(seg):
        return pl.BlockSpec((S, dv), lambda b, h, i: (b, seg * per_seg + h))

    est = (2 * 2 * S * dv * 2 + S * (dv + ones_rows) * 2 + 2 * 3 * bq * dv * 2
           + (dv + ones_rows + 1) * 2 * bq * 4 + 2 * bk * 2 * bq * 4 + cs * cs * 4 + 8 * LOOP_BLOCKS * bk * cs * 4
           + 6 * bq * dv * 4)
    return pl.pallas_call(
        functools.partial(_diff_kernel, lam_init=lam_init, cs=cs),
        grid=(B, H, nq),
        in_specs=[
            qcol(SEG_QD), kvcol(SEG_KD), kvcol(SEG_VD), qcol(SEG_GD),
            pl.BlockSpec((4, DIFF_HEAD_DIM), lambda b, h, i: (0, 0)),
            pl.BlockSpec((dv, 1), lambda b, h, i: (0, 0)),
        ],
        out_specs=pl.BlockSpec((bq, dv), lambda b, h, i: (b * nq + i, h)),
        out_shape=jax.ShapeDtypeStruct((T, DIFF_WIDTH), BF16),
        scratch_shapes=[
            pltpu.VMEM((S // bk, dv + ones_rows, bk), BF16),
            pltpu.VMEM((1, 2 * bq), F32),
            pltpu.VMEM((dv + ones_rows, 2 * bq), F32),
            pltpu.VMEM((2, bk, 2 * bq), F32),
            pltpu.VMEM((cs, cs), F32),
            pltpu.VMEM((2 * bq, dv), BF16),
        ],
        compiler_params=pltpu.CompilerParams(
            dimension_semantics=("parallel", "parallel", "arbitrary"),
            vmem_limit_bytes=_vmem_limit(est)),
        name="diff_attn",
    )(proj, proj, proj, proj, dl, sub)


def _merge_kernel(ya_ref, yb_ref, yc_ref, wa_ref, wb_ref, wc_ref, g0_ref, g1_ref, g2_ref, o_ref):
    for c in range(o_ref.shape[1] // MXU_DIM):
        cols = slice(c * MXU_DIM, (c + 1) * MXU_DIM)
        m = g0_ref[:, cols].astype(F32) * jnp.dot(ya_ref[...], wa_ref[:, cols],
                                                  preferred_element_type=F32)
        m = m + g1_ref[:, cols].astype(F32) * jnp.dot(yb_ref[...], wb_ref[:, cols],
                                                      preferred_element_type=F32)
        m = m + g2_ref[:, cols].astype(F32) * jnp.dot(yc_ref[...], wc_ref[:, cols],
                                                      preferred_element_type=F32)
        o_ref[:, cols] = m.astype(BF16)


def _merge(ya, yb, yc, wa, wb, wc, proj, bm=512):
    T, W = ya.shape
    D = wa.shape[1]
    ybk = pl.BlockSpec((bm, W), lambda i: (i, 0))
    wbk = pl.BlockSpec((W, D), lambda i: (0, 0), pipeline_mode=pl.Buffered(1))
    gm0 = SEG_GM * SEG // D

    def gate(br):
        return pl.BlockSpec((bm, D), lambda i: (i, gm0 + br))

    est = 2 * 3 * bm * W * 2 + 3 * W * D * 2 + 2 * 4 * bm * D * 2 + 8 * bm * MXU_DIM * 4
    return pl.pallas_call(
        _merge_kernel,
        grid=(T // bm,),
        in_specs=[ybk, ybk, ybk, wbk, wbk, wbk, gate(0), gate(1), gate(2)],
        out_specs=pl.BlockSpec((bm, D), lambda i: (i, 0)),
        out_shape=jax.ShapeDtypeStruct((T, D), BF16),
        compiler_params=pltpu.CompilerParams(
            dimension_semantics=("parallel",),
            vmem_limit_bytes=_vmem_limit(est)),
        name="merge",
    )(ya, yb, yc, wa, wb, wc, proj, proj, proj)


def _out_kernel(m_ref, w_ref, x_ref, g_ref, o_ref):
    for r in range(o_ref.shape[0] // OUT_ROWS):
        rows = slice(r * OUT_ROWS, (r + 1) * OUT_ROWS)
        o = jnp.dot(m_ref[rows, :], w_ref[...], preferred_element_type=F32)
        y = o * lax.rsqrt(jnp.mean(o * o, axis=-1, keepdims=True) + EPS)
        o_ref[rows, :] = x_ref[rows, :] + y * g_ref[...]


def _out_proj(m, w, x2, g, bm=512):
    T, D = x2.shape
    est = 2 * bm * D * 2 + D * D * 2 + 4 * bm * D * 4 + 6 * OUT_ROWS * D * 4
    return pl.pallas_call(
        _out_kernel,
        grid=(T // bm,),
        in_specs=[
            pl.BlockSpec((bm, D), lambda i: (i, 0)),
            pl.BlockSpec((D, D), lambda i: (0, 0), pipeline_mode=pl.Buffered(1)),
            pl.BlockSpec((bm, D), lambda i: (i, 0)),
            pl.BlockSpec((1, D), lambda i: (0, 0)),
        ],
        out_specs=pl.BlockSpec((bm, D), lambda i: (i, 0)),
        out_shape=jax.ShapeDtypeStruct((T, D), F32),
        compiler_params=pltpu.CompilerParams(
            dimension_semantics=("parallel",),
            vmem_limit_bytes=_vmem_limit(est)),
        name="out_proj",
    )(m, w, x2, g)


def _rope_tables(S):
    pos = jnp.arange(S, dtype=F32)
    ret_freq = 1.0 / (RET_ROT_BASE ** jnp.linspace(0.0, 1.0, RET_HEAD_DIM // 2, dtype=F32))
    ang = pos[:, None] * ret_freq[None, :]
    c, s = jnp.cos(ang), jnp.sin(ang)
    zero = jnp.zeros_like(s)
    rc = jnp.stack([c, c], axis=-1).reshape(S, RET_HEAD_DIM)
    rm = jnp.stack([-s, zero], axis=-1).reshape(S, RET_HEAD_DIM)
    rp = jnp.stack([zero, s], axis=-1).reshape(S, RET_HEAD_DIM)
    inv_freq = ROPE_THETA ** (-jnp.arange(0, ROPE_DIM, 2, dtype=F32) / ROPE_DIM)
    angd = pos[:, None] * inv_freq[None, :]
    cd, sd = jnp.cos(angd), jnp.sin(angd)
    half = ROPE_DIM // 2
    rest = DIFF_HEAD_DIM - ROPE_DIM
    one = jnp.ones((S, rest), F32)
    zero_r = jnp.zeros((S, rest), F32)
    zero_h = jnp.zeros((S, half), F32)
    da = jnp.concatenate([cd, cd, one], axis=-1)
    dm = jnp.concatenate([-sd, zero_h, zero_r], axis=-1)
    dp = jnp.concatenate([zero_h, sd, zero_r], axis=-1)
    rep = LANES // DIFF_HEAD_DIM
    return rc, rm, rp, jnp.tile(da, (1, rep)), jnp.tile(dm, (1, rep)), jnp.tile(dp, (1, rep))


def _decay_tables():
    C = RET_CHUNK
    H = RET_HEADS
    scale = RET_HEAD_DIM ** -0.5
    log_g = jnp.log1p(-jnp.exp2(-5.0 - jnp.arange(H, dtype=F32)))
    idx = jnp.arange(C, dtype=F32)
    rel = idx[:, None] - idx[None, :]
    intra = jnp.where(rel[None] >= 0,
                      jnp.exp(log_g[:, None, None] * jnp.maximum(rel, 0.0)[None]), 0.0) * scale
    k_decay = jnp.exp(log_g[:, None] * (C - 1.0 - idx)[None, :]) * scale
    q_decay = jnp.exp(log_g[:, None] * (idx + 1.0)[None, :])
    kdec = jnp.broadcast_to(k_decay[:, :, None], (H, C, RET_HEAD_DIM))
    qdec = jnp.broadcast_to(q_decay[:, :, None], (H, C, RET_HEAD_DIM))
    return intra, kdec, qdec


def _gate_weights(wa, wx):
    per = MXU_DIM // LRU_BLOCK
    nt = LRU_BLOCKS // per

    def dense(w):
        w4 = w.reshape(nt, per, LRU_BLOCK, LRU_BLOCK)
        eye = jnp.eye(per, dtype=w.dtype)
        d = jnp.einsum('tpde,pq->tpdqe', w4, eye)
        return d.reshape(nt, MXU_DIM, MXU_DIM)

    return jnp.concatenate([dense(wa), dense(wx)], axis=-1).astype(BF16)


def _cast_kernel(w_ref, o_ref):
    o_ref[...] = w_ref[...].astype(BF16)


def _cast_layer_weight(w, l, br=256, bc=4096):
    _, R, C = w.shape
    br, bc = min(br, R), min(bc, C)
    assert R % br == 0 and C % bc == 0
    return pl.pallas_call(
        _cast_kernel,
        grid=(R // br, C // bc),
        in_specs=[pl.BlockSpec((None, br, bc), lambda i, j: (l, i, j))],
        out_specs=pl.BlockSpec((br, bc), lambda i, j: (i, j)),
        out_shape=jax.ShapeDtypeStruct((R, C), BF16),
        compiler_params=pltpu.CompilerParams(
            dimension_semantics=("parallel", "parallel"),
            vmem_limit_bytes=_vmem_limit(2 * br * bc * (4 + 2) + br * bc * 4)),
        name="cast_weight",
    )(w)


def kernel(x, pre_norm, post_norm, w_in, conv_w, conv_b, lru_wa, lru_ba, lru_wx, lru_bx,
           lru_lambda, diff_lambda, diff_subln, w_branch_a, w_branch_b, w_branch_c, w_out):
    B, S, D = x.shape
    depth = w_in.shape[0]
    T = B * S
    tabs = _rope_tables(S)
    intra, kdec, qdec = _decay_tables()
    x2 = x.reshape(T, D)
    for l in range(depth):
        lam_init = 0.8 - 0.6 * math.exp(-0.3 * l)
        proj = _in_proj(x2, pre_norm[l][None, :], _cast_layer_weight(w_in, l), tabs, S)
        ya, yb = _lru_retention(
            proj, conv_w[l], conv_b[l][None, :], _gate_weights(lru_wa[l], lru_wx[l]),
            lru_ba[l][None, :], lru_bx[l][None, :], lru_lambda[l][None, :], intra, kdec, qdec, B, S)
        yc = _diff_attn(proj, diff_lambda[l], diff_subln[l][:, None], lam_init, B, S)
        m = _merge(ya, yb, yc, w_branch_a[l].astype(BF16), w_branch_b[l].astype(BF16),
                   w_branch_c[l].astype(BF16), proj)
        x2 = _out_proj(m, w_out[l].astype(BF16), x2, post_norm[l][None, :])
    return x2.reshape(B, S, D)
```

```python
import functools
import math

import jax
import jax.numpy as jnp
from jax import lax
from jax.experimental import pallas as pl
from jax.experimental.pallas import tpu as pltpu

F32 = jnp.float32
BF16 = jnp.bfloat16

D_MODEL = 2048
LRU_WIDTH = 1024
LRU_BLOCKS = 16
LRU_BLOCK = LRU_WIDTH // LRU_BLOCKS
CONV_WIDTH = 4
LRU_C = 8.0
RET_HEADS = 8
RET_HEAD_DIM = 128
RET_WIDTH = RET_HEADS * RET_HEAD_DIM
RET_CHUNK = 128
RET_ROT_BASE = 10000.0
DIFF_HEADS = 8
DIFF_HEAD_DIM = 64
DIFF_V_DIM = 2 * DIFF_HEAD_DIM
DIFF_WIDTH = DIFF_HEADS * DIFF_V_DIM
ROPE_THETA = 500000.0
ROPE_DIM = DIFF_HEAD_DIM // 4
N_BRANCH = 3
EPS = 1e-6

LANES = 128
SUBLANES = 8
MXU_DIM = 256
VMEM_BYTES_V7X = 64 * 1024 * 1024
VMEM_RESERVE = 8 * 1024 * 1024
VMEM_MIN_REQUEST = 16 * 1024 * 1024

SEG = 1024
NORM_ROWS = 256
OUT_ROWS = 256
PROJ_ROWS = 256
LOOP_BLOCKS = 4
SEG_XA, SEG_GA, SEG_QR, SEG_KR, SEG_VR, SEG_GR, SEG_QD, SEG_KD, SEG_VD, SEG_GD, SEG_GM = range(11)
SEG_KINDS = ("plain", "silu", "ret", "ret", "plain", "silu", "dif", "dif", "plain", "silu") + (
    "sigmoid",) * (N_BRANCH * D_MODEL // SEG)


def _sigmoid(x):
    return 1.0 / (1.0 + jnp.exp(-x))


def _vmem_limit(nbytes):
    return int(min(max(nbytes, VMEM_MIN_REQUEST), VMEM_BYTES_V7X - VMEM_RESERVE))


def _in_proj_kernel(x_ref, g_ref, w_ref, rc_ref, rm_ref, rp_ref, da_ref, dm_ref, dp_ref, o_ref, h_ref):
    j = pl.program_id(1)

    def normalise():
        for r in range(x_ref.shape[0] // NORM_ROWS):
            rows = slice(r * NORM_ROWS, (r + 1) * NORM_ROWS)
            xf = x_ref[rows, :]
            ms = jnp.mean(xf * xf, axis=-1, keepdims=True)
            h_ref[rows, :] = ((xf * lax.rsqrt(ms + EPS)) * g_ref[...]).astype(BF16)

    def project(epilogues):
        bm, bn = o_ref.shape
        rsub = min(bm, PROJ_ROWS)
        for c in range(bn // MXU_DIM):
            cols = slice(c * MXU_DIM, (c + 1) * MXU_DIM)
            epilogue = epilogues[c * MXU_DIM // SEG]
            for r in range(bm // rsub):
                rows = slice(r * rsub, (r + 1) * rsub)
                acc = jnp.dot(h_ref[rows, :], w_ref[:, cols], preferred_element_type=F32)
                o_ref[rows, cols] = epilogue(acc, rows).astype(BF16)

    def per_lane_chunk(fn):
        def epilogue(acc, rows):
            return jnp.concatenate(
                [fn(acc[:, c * LANES:(c + 1) * LANES], rows) for c in range(MXU_DIM // LANES)],
                axis=1)
        return epilogue

    half = ROPE_DIM // 2
    epilogue_of = {
        "plain": lambda acc, rows: acc,
        "silu": lambda acc, rows: acc * _sigmoid(acc),
        "sigmoid": lambda acc, rows: _sigmoid(acc),
        "ret": per_lane_chunk(
            lambda xc, rows: xc * rc_ref[rows, :] + pltpu.roll(xc, LANES - 1, 1) * rm_ref[rows, :]
            + pltpu.roll(xc, 1, 1) * rp_ref[rows, :]),
        "dif": per_lane_chunk(
            lambda xc, rows: xc * da_ref[rows, :] + pltpu.roll(xc, LANES - half, 1) * dm_ref[rows, :]
            + pltpu.roll(xc, half, 1) * dp_ref[rows, :]),
    }
    nseg = o_ref.shape[1] // SEG

    @pl.when(j == 0)
    def _():
        normalise()

    blocks = {}
    for jv in range(len(SEG_KINDS) // nseg):
        blocks.setdefault(SEG_KINDS[jv * nseg:(jv + 1) * nseg], []).append(jv)
    for kinds, jvs in blocks.items():
        cond = functools.reduce(jnp.logical_or, [j == jv for jv in jvs])

        @pl.when(cond)
        def _(kinds=kinds):
            project([epilogue_of[kind] for kind in kinds])


def _in_proj(x2, g, w, tabs, S, bm=1024, bn=2 * SEG):
    T, D = x2.shape
    N = w.shape[1]
    assert bn % SEG == 0 and T % bm == 0 and S % bm == 0 and N % bn == 0
    nb = S // bm
    tab_spec = pl.BlockSpec((bm, LANES), lambda i, j: (i % nb, 0))
    est = (2 * bm * D * 4 + bm * D * 2 + 2 * D * bn * 2 + 2 * bm * bn * 2
           + 12 * bm * LANES * 4 + 8 * bm * MXU_DIM * 4 + 4 * NORM_ROWS * D * 4)
    return pl.pallas_call(
        _in_proj_kernel,
        grid=(T // bm, N // bn),
        in_specs=[
            pl.BlockSpec((bm, D), lambda i, j: (i, 0)),
            pl.BlockSpec((1, D), lambda i, j: (0, 0)),
            pl.BlockSpec((D, bn), lambda i, j: (0, j)),
            tab_spec, tab_spec, tab_spec, tab_spec, tab_spec, tab_spec,
        ],
        out_specs=pl.BlockSpec((bm, bn), lambda i, j: (i, j)),
        out_shape=jax.ShapeDtypeStruct((T, N), BF16),
        scratch_shapes=[pltpu.VMEM((bm, D), BF16)],
        compiler_params=pltpu.CompilerParams(
            dimension_semantics=("parallel", "arbitrary"),
            vmem_limit_bytes=_vmem_limit(est)),
        name="in_proj",
    )(x2, g, w, *tabs)


def _lru_kernel(xa_ref, ga_ref, cw_ref, cb_ref, wg_ref, ba_ref, bx_ref, lam_ref, o_ref,
                xext, a_s, u_s, hc, alongside=None):
    nb, tc, _ = xa_ref.shape
    pad = SUBLANES

    z = -lam_ref[...]
    sp = jnp.maximum(z, 0.0) + jnp.log1p(jnp.exp(-jnp.abs(z)))
    gw = 2 * MXU_DIM
    for bi in range(nb):
        xext[bi, pad:pad + tc, :] = xa_ref[bi].astype(F32)
        xc = cb_ref[...] + cw_ref[CONV_WIDTH - 1:CONV_WIDTH, :] * xext[bi, pad:pad + tc, :]
        for k in range(CONV_WIDTH - 1):
            off = pad - (CONV_WIDTH - 1) + k
            xc = xc + cw_ref[k:k + 1, :] * xext[bi, off:off + tc, :]
        xext[bi, 0:pad, :] = xext[bi, tc:tc + pad, :]

        xb = xc.astype(BF16)
        for c in range(LRU_WIDTH // MXU_DIM):
            nparts = 2 * (LRU_WIDTH // MXU_DIM)
            if alongside is not None:
                alongside(bi, 2 * c, nparts)
            sl = slice(c * MXU_DIM, (c + 1) * MXU_DIM)
            g = jnp.dot(xb[:, sl], wg_ref[c], preferred_element_type=F32)
            r = _sigmoid(g[:, 0:MXU_DIM] + ba_ref[:, sl])
            ig = _sigmoid(g[:, MXU_DIM:gw] + bx_ref[:, sl])
            log_a = (-LRU_C * r) * sp[:, sl]
            a = jnp.exp(log_a)
            a_s[bi, :, sl] = a
            if alongside is not None:
                alongside(bi, 2 * c + 1, nparts)
            u_s[bi, :, sl] = jnp.sqrt(-jnp.tanh(log_a) * (a * a + 1.0)) * (ig * xc[:, sl])

    def step(t, hs):
        out = []
        for bi in range(nb):
            h = a_s[bi, pl.ds(t, 1), :] * hs[bi] + u_s[bi, pl.ds(t, 1), :]
            u_s[bi, pl.ds(t, 1), :] = h
            out.append(h)
        return tuple(out)

    hs = lax.fori_loop(0, tc, step, tuple(hc[bi] for bi in range(nb)), unroll=8)
    for bi in range(nb):
        hc[bi] = hs[bi]
        o_ref[bi] = (u_s[bi] * ga_ref[bi].astype(F32)).astype(BF16)


def _lru_ret_kernel(xa_ref, ga_ref, cw_ref, cb_ref, wg_ref, ba_ref, bx_ref, lam_ref,
                    q_ref, k_ref, v_ref, g_ref, intra_ref, kdec_ref, qdec_ref,
                    ya_ref, yb_ref, xext, a_s, u_s, hc, st_ref):
    nb = xa_ref.shape[0]

    @pl.when(pl.program_id(1) == 0)
    def _():
        xext[:, 0:SUBLANES, :] = jnp.zeros((nb, SUBLANES, LRU_WIDTH), F32)
        hc[...] = jnp.zeros_like(hc)
        st_ref[...] = jnp.zeros_like(st_ref)

    def retention(bi, part, nparts):
        _ret_kernel(q_ref.at[bi], k_ref.at[bi], v_ref.at[bi], g_ref.at[bi],
                    intra_ref, kdec_ref, qdec_ref, yb_ref.at[bi], st_ref.at[bi], part, nparts)

    _lru_kernel(xa_ref, ga_ref, cw_ref, cb_ref, wg_ref, ba_ref, bx_ref, lam_ref, ya_ref,
                xext, a_s, u_s, hc, alongside=retention)


def _lru_retention(proj, cw, cb, wg, ba, bx, lam, intra, kdec, qdec, B, S, tc=256, nb=4):
    T = proj.shape[0]
    W = LRU_WIDTH
    C = RET_CHUNK
    assert B % nb == 0 and S % tc == 0 and tc % C == 0 and RET_WIDTH == W
    proj3 = proj.reshape(B, S, proj.shape[1])
    vec = pl.BlockSpec((1, W), lambda b, n: (0, 0))

    def col(seg):
        return pl.BlockSpec((nb, tc, W), lambda b, n: (b, n, seg))

    def table(cols):
        return pl.BlockSpec((RET_HEADS, C, cols), lambda b, n: (0, 0, 0))

    est = (2 * 8 * nb * tc * W * 2 + 3 * nb * tc * W * 4 + 8 * tc * W * 4
           + 2 * RET_HEADS * C * (C + 2 * RET_HEAD_DIM) * 4
           + nb * RET_HEADS * RET_HEAD_DIM * RET_HEAD_DIM * 4 + 16 * RET_HEADS * C * C * 4)
    ya, yb = pl.pallas_call(
        _lru_ret_kernel,
        grid=(B // nb, S // tc),
        in_specs=[
            col(SEG_XA), col(SEG_GA),
            pl.BlockSpec((CONV_WIDTH, W), lambda b, n: (0, 0)),
            vec,
            pl.BlockSpec((W // MXU_DIM, MXU_DIM, 2 * MXU_DIM), lambda b, n: (0, 0, 0)),
            vec, vec, vec,
            col(SEG_QR), col(SEG_KR), col(SEG_VR), col(SEG_GR),
            table(C), table(RET_HEAD_DIM), table(RET_HEAD_DIM),
        ],
        out_specs=[pl.BlockSpec((nb, tc, W), lambda b, n: (b, n, 0)),
                   pl.BlockSpec((nb, tc, W), lambda b, n: (b, n, 0))],
        out_shape=[jax.ShapeDtypeStruct((B, S, W), BF16), jax.ShapeDtypeStruct((B, S, W), BF16)],
        scratch_shapes=[
            pltpu.VMEM((nb, tc + SUBLANES, W), F32),
            pltpu.VMEM((nb, tc, W), F32),
            pltpu.VMEM((nb, tc, W), F32),
            pltpu.VMEM((nb, 1, W), F32),
            pltpu.VMEM((nb, RET_HEADS, RET_HEAD_DIM, RET_HEAD_DIM), F32),
        ],
        compiler_params=pltpu.CompilerParams(
            dimension_semantics=("parallel", "arbitrary"),
            vmem_limit_bytes=_vmem_limit(est)),
        name="lru_retention",
    )(proj3, proj3, cw, cb, wg, ba, bx, lam, proj3, proj3, proj3, proj3, intra, kdec, qdec)
    return ya.reshape(T, W), yb.reshape(T, W)


def _ret_kernel(q_ref, k_ref, v_ref, g_ref, intra_ref, kdec_ref, qdec_ref, o_ref, st_ref,
                part=0, nparts=1):
    tc = q_ref.shape[0]
    C = RET_CHUNK
    dh = RET_HEAD_DIM
    nunits = (tc // C) * RET_HEADS
    assert nunits % nparts == 0
    lo, hi = part * nunits // nparts, (part + 1) * nunits // nparts

    nt_dims = (((1,), (1,)), ((), ()))
    for c in range(tc // C):
        rows = slice(c * C, (c + 1) * C)
        for h in range(RET_HEADS):
            if not lo <= c * RET_HEADS + h < hi:
                continue
            sl = slice(h * dh, (h + 1) * dh)
            intra = intra_ref[h]
            kdec = kdec_ref[h]
            qdec = qdec_ref[h]
            cdec = qdec[C - 1:C, :]
            q = q_ref[rows, sl]
            k = k_ref[rows, sl]
            v = v_ref[rows, sl]
            state = st_ref[h]
            scores = lax.dot_general(q, k, nt_dims, preferred_element_type=F32) * intra
            inner = jnp.dot(scores.astype(BF16), v, preferred_element_type=F32)
            qd = (q.astype(F32) * qdec).astype(BF16)
            cross = jnp.dot(qd, state.astype(BF16), preferred_element_type=F32)
            kd = (k.astype(F32) * kdec).astype(BF16)
            kv = lax.dot_general(kd, v, (((0,), (0,)), ((), ())),
                                 preferred_element_type=F32)
            st_ref[h] = state * cdec + kv
            out = inner + cross
            mu = jnp.mean(out, axis=-1, keepdims=True)
            d = out - mu
            var = jnp.mean(d * d, axis=-1, keepdims=True)
            y = d * lax.rsqrt(var + EPS)
            o_ref[rows, sl] = (y * g_ref[rows, sl].astype(F32)).astype(BF16)


def _diff_kernel(q_ref, k_ref, v_ref, g_ref, dl_ref, sub_ref, o_ref, vxt_ref, m_ref, a_ref, s_ref,
                 tri_ref, qq_ref, *, lam_init, cs):
    qi = pl.program_id(2)
    bq = q_ref.shape[0]
    nkb, vx_rows, bk = vxt_ref.shape
    d = DIFF_HEAD_DIM
    dv = 2 * d
    nt_dims = (((1,), (1,)), ((), ()))

    @pl.when(qi == 0)
    def _():
        for c in range(nkb):
            vxt_ref[c, 0:dv, :] = v_ref[c * bk:(c + 1) * bk, :].astype(F32).T.astype(BF16)
            vxt_ref[c, dv:vx_rows, :] = jnp.ones((vx_rows - dv, bk), BF16)

    def prep_queries(c):
        lane = lax.broadcasted_iota(jnp.int32, (cs, dv), 1)
        q0 = (c * cs) % bq
        qs = q_ref[q0:q0 + cs, :].astype(F32) * (d ** -0.5 * math.log2(math.e))
        keep = (lane < d) if c * cs < bq else (lane >= d)
        qq_ref[c * cs:(c + 1) * cs, :] = jnp.where(keep, qs, 0.0).astype(BF16)

    ncol = 2 * bq // cs

    tri_ref[...] = jnp.where(lax.broadcasted_iota(jnp.int32, (cs, cs), 0)
                             <= lax.broadcasted_iota(jnp.int32, (cs, cs), 1), 0.0, -jnp.inf)

    def key_plan(c, kv0):
        nsubk = bk // cs
        if kv0 is None:
            return nsubk, False
        delta = (c * cs) % bq - kv0
        if delta < 0:
            return 0, False
        nfullk = min(nsubk, delta // cs)
        return nfullk, nfullk < nsubk

    def scores(t, slot, kv0=None, first=False):
        k = k_ref[pl.ds(pl.multiple_of(t * bk, bk), bk), :]
        for c in range(ncol):
            if first:
                prep_queries(c)
            nfullk, tri = key_plan(c, kv0)
            used = (nfullk + tri) * cs
            if used:
                s_ref[slot, 0:used, c * cs:(c + 1) * cs] = lax.dot_general(
                    k[0:used], qq_ref[c * cs:(c + 1) * cs, :], nt_dims,
                    preferred_element_type=F32)

    def softmax_pv(t, slot, kv0):
        vxt = vxt_ref[t]
        for c in range(ncol):
            nfullk, tri = key_plan(c, kv0)
            used = (nfullk + tri) * cs
            if not used:
                continue
            cols = slice(c * cs, (c + 1) * cs)
            parts = []
            if nfullk:
                parts.append(s_ref[slot, 0:nfullk * cs, cols])
            if tri:
                parts.append(s_ref[slot, nfullk * cs:used, cols] + tri_ref[...])
            m_new = m_prev = m_ref[:, cols]
            for s in parts:
                m_new = jnp.maximum(m_new, jnp.max(s, axis=0, keepdims=True))
            alpha = jnp.exp2(m_prev - m_new)
            p = [jnp.exp2(s - m_new).astype(BF16) for s in parts]
            p = p[0] if len(p) == 1 else jnp.concatenate(p, axis=0)
            pv = jnp.dot(vxt[:, 0:used], p, preferred_element_type=F32)
            a_ref[:, cols] = alpha * a_ref[:, cols] + pv
            m_ref[:, cols] = m_new

    ndiag = bq // bk
    nfull = qi * ndiag

    def body(u, carry):
        for i in range(LOOP_BLOCKS):
            t = LOOP_BLOCKS * u + i
            scores(t + 1, (i + 1) % 2)
            softmax_pv(t, i % 2, None)
        return carry

    scores(0, 0, first=True)
    m_ref[...] = jnp.full(m_ref.shape, -1e30, F32)
    a_ref[...] = jnp.zeros(a_ref.shape, F32)
    lax.fori_loop(0, nfull // LOOP_BLOCKS, body, 0)
    for jj in range(ndiag):
        if jj + 1 < ndiag:
            scores(nfull + jj + 1, (jj + 1) % 2, (jj + 1) * bk)
        softmax_pv(nfull + jj, jj % 2, jj * bk)

    dl = dl_ref[...].astype(F32)
    lam = (jnp.exp(jnp.sum(dl[0:1, :] * dl[1:2, :], axis=-1, keepdims=True))
           - jnp.exp(jnp.sum(dl[2:3, :] * dl[3:4, :], axis=-1, keepdims=True)) + lam_init)
    o1 = a_ref[0:dv, 0:bq] / a_ref[dv:dv + 1, 0:bq]
    o2 = a_ref[0:dv, bq:2 * bq] / a_ref[dv:dv + 1, bq:2 * bq]
    of = o1 - lam * o2
    of = of * lax.rsqrt(jnp.mean(of * of, axis=0, keepdims=True) + EPS) * sub_ref[...]
    o = (of * (1.0 - lam_init)).T
    o_ref[...] = (o * g_ref[...].astype(F32)).astype(BF16)


def _diff_attn(proj, dl, sub, lam_init, B, S, bq=2048, bk=512, cs=256):
    T = proj.shape[0]
    nq = S // bq
    H = DIFF_HEADS
    dv = DIFF_V_DIM
    per_seg = SEG // dv
    ones_rows = 2 * SUBLANES
    assert bq % (LOOP_BLOCKS * bk) == 0 and LOOP_BLOCKS % 2 == 0 and S % bq == 0 and bk % cs == 0

    def qcol(seg):
        return pl.BlockSpec((bq, dv), lambda b, h, i: (b * nq + i, seg * per_seg + h))

    def kvcol(seg):
        return pl.BlockSpec((S, dv), lambda b, h, i: (b, seg * per_seg + h))

    est = (2 * 2 * S * dv * 2 + S * (dv + ones_rows) * 2 + 2 * 3 * bq * dv * 2
           + (dv + ones_rows + 1) * 2 * bq * 4 + 2 * bk * 2 * bq * 4 + cs * cs * 4 + 8 * LOOP_BLOCKS * bk * cs * 4
           + 6 * bq * dv * 4)
    return pl.pallas_call(
        functools.partial(_diff_kernel, lam_init=lam_init, cs=cs),
        grid=(B, H, nq),
        in_specs=[
            qcol(SEG_QD), kvcol(SEG_KD), kvcol(SEG_VD), qcol(SEG_GD),
            pl.BlockSpec((4, DIFF_HEAD_DIM), lambda b, h, i: (0, 0)),
            pl.BlockSpec((dv, 1), lambda b, h, i: (0, 0)),
        ],
        out_specs=pl.BlockSpec((bq, dv), lambda b, h, i: (b * nq + i, h)),
        out_shape=jax.ShapeDtypeStruct((T, DIFF_WIDTH), BF16),
        scratch_shapes=[
            pltpu.VMEM((S // bk, dv + ones_rows, bk), BF16),
            pltpu.VMEM((1, 2 * bq), F32),
            pltpu.VMEM((dv + ones_rows, 2 * bq), F32),
            pltpu.VMEM((2, bk, 2 * bq), F32),
            pltpu.VMEM((cs, cs), F32),
            pltpu.VMEM((2 * bq, dv), BF16),
        ],
        compiler_params=pltpu.CompilerParams(
            dimension_semantics=("parallel", "parallel", "arbitrary"),
            vmem_limit_bytes=_vmem_limit(est)),
        name="diff_attn",
    )(proj, proj, proj, proj, dl, sub)


def _merge_kernel(ya_ref, yb_ref, yc_ref, wa_ref, wb_ref, wc_ref, g0_ref, g1_ref, g2_ref, o_ref):
    for c in range(o_ref.shape[1] // MXU_DIM):
        cols = slice(c * MXU_DIM, (c + 1) * MXU_DIM)
        m = g0_ref[:, cols].astype(F32) * jnp.dot(ya_ref[...], wa_ref[:, cols],
                                                  preferred_element_type=F32)
        m = m + g1_ref[:, cols].astype(F32) * jnp.dot(yb_ref[...], wb_ref[:, cols],
                                                      preferred_element_type=F32)
        m = m + g2_ref[:, cols].astype(F32) * jnp.dot(yc_ref[...], wc_ref[:, cols],
                                                      preferred_element_type=F32)
        o_ref[:, cols] = m.astype(BF16)


def _merge(ya, yb, yc, wa, wb, wc, proj, bm=512):
    T, W = ya.shape
    D = wa.shape[1]
    ybk = pl.BlockSpec((bm, W), lambda i: (i, 0))
    wbk = pl.BlockSpec((W, D), lambda i: (0, 0), pipeline_mode=pl.Buffered(1))
    gm0 = SEG_GM * SEG // D

    def gate(br):
        return pl.BlockSpec((bm, D), lambda i: (i, gm0 + br))

    est = 2 * 3 * bm * W * 2 + 3 * W * D * 2 + 2 * 4 * bm * D * 2 + 8 * bm * MXU_DIM * 4
    return pl.pallas_call(
        _merge_kernel,
        grid=(T // bm,),
        in_specs=[ybk, ybk, ybk, wbk, wbk, wbk, gate(0), gate(1), gate(2)],
        out_specs=pl.BlockSpec((bm, D), lambda i: (i, 0)),
        out_shape=jax.ShapeDtypeStruct((T, D), BF16),
        compiler_params=pltpu.CompilerParams(
            dimension_semantics=("parallel",),
            vmem_limit_bytes=_vmem_limit(est)),
        name="merge",
    )(ya, yb, yc, wa, wb, wc, proj, proj, proj)


def _out_kernel(m_ref, w_ref, x_ref, g_ref, o_ref):
    for r in range(o_ref.shape[0] // OUT_ROWS):
        rows = slice(r * OUT_ROWS, (r + 1) * OUT_ROWS)
        o = jnp.dot(m_ref[rows, :], w_ref[...], preferred_element_type=F32)
        y = o * lax.rsqrt(jnp.mean(o * o, axis=-1, keepdims=True) + EPS)
        o_ref[rows, :] = x_ref[rows, :] + y * g_ref[...]


def _out_proj(m, w, x2, g, bm=512):
    T, D = x2.shape
    est = 2 * bm * D * 2 + D * D * 2 + 4 * bm * D * 4 + 6 * OUT_ROWS * D * 4
    return pl.pallas_call(
        _out_kernel,
        grid=(T // bm,),
        in_specs=[
            pl.BlockSpec((bm, D), lambda i: (i, 0)),
            pl.BlockSpec((D, D), lambda i: (0, 0), pipeline_mode=pl.Buffered(1)),
            pl.BlockSpec((bm, D), lambda i: (i, 0)),
            pl.BlockSpec((1, D), lambda i: (0, 0)),
        ],
        out_specs=pl.BlockSpec((bm, D), lambda i: (i, 0)),
        out_shape=jax.ShapeDtypeStruct((T, D), F32),
        compiler_params=pltpu.CompilerParams(
            dimension_semantics=("parallel",),
            vmem_limit_bytes=_vmem_limit(est)),
        name="out_proj",
    )(m, w, x2, g)


def _rope_tables(S):
    pos = jnp.arange(S, dtype=F32)
    ret_freq = 1.0 / (RET_ROT_BASE ** jnp.linspace(0.0, 1.0, RET_HEAD_DIM // 2, dtype=F32))
    ang = pos[:, None] * ret_freq[None, :]
    c, s = jnp.cos(ang), jnp.sin(ang)
    zero = jnp.zeros_like(s)
    rc = jnp.stack([c, c], axis=-1).reshape(S, RET_HEAD_DIM)
    rm = jnp.stack([-s, zero], axis=-1).reshape(S, RET_HEAD_DIM)
    rp = jnp.stack([zero, s], axis=-1).reshape(S, RET_HEAD_DIM)
    inv_freq = ROPE_THETA ** (-jnp.arange(0, ROPE_DIM, 2, dtype=F32) / ROPE_DIM)
    angd = pos[:, None] * inv_freq[None, :]
    cd, sd = jnp.cos(angd), jnp.sin(angd)
    half = ROPE_DIM // 2
    rest = DIFF_HEAD_DIM - ROPE_DIM
    one = jnp.ones((S, rest), F32)
    zero_r = jnp.zeros((S, rest), F32)
    zero_h = jnp.zeros((S, half), F32)
    da = jnp.concatenate([cd, cd, one], axis=-1)
    dm = jnp.concatenate([-sd, zero_h, zero_r], axis=-1)
    dp = jnp.concatenate([zero_h, sd, zero_r], axis=-1)
    rep = LANES // DIFF_HEAD_DIM
    return rc, rm, rp, jnp.tile(da, (1, rep)), jnp.tile(dm, (1, rep)), jnp.tile(dp, (1, rep))


def _decay_tables():
    C = RET_CHUNK
    H = RET_HEADS
    scale = RET_HEAD_DIM ** -0.5
    log_g = jnp.log1p(-jnp.exp2(-5.0 - jnp.arange(H, dtype=F32)))
    idx = jnp.arange(C, dtype=F32)
    rel = idx[:, None] - idx[None, :]
    intra = jnp.where(rel[None] >= 0,
                      jnp.exp(log_g[:, None, None] * jnp.maximum(rel, 0.0)[None]), 0.0) * scale
    k_decay = jnp.exp(log_g[:, None] * (C - 1.0 - idx)[None, :]) * scale
    q_decay = jnp.exp(log_g[:, None] * (idx + 1.0)[None, :])
    kdec = jnp.broadcast_to(k_decay[:, :, None], (H, C, RET_HEAD_DIM))
    qdec = jnp.broadcast_to(q_decay[:, :, None], (H, C, RET_HEAD_DIM))
    return intra, kdec, qdec


def _gate_weights(wa, wx):
    per = MXU_DIM // LRU_BLOCK
    nt = LRU_BLOCKS // per

    def dense(w):
        w4 = w.reshape(nt, per, LRU_BLOCK, LRU_BLOCK)
        eye = jnp.eye(per, dtype=w.dtype)
        d = jnp.einsum('tpde,pq->tpdqe', w4, eye)
        return d.reshape(nt, MXU_DIM, MXU_DIM)

    return jnp.concatenate([dense(wa), dense(wx)], axis=-1).astype(BF16)


def _cast_kernel(w_ref, o_ref):
    o_ref[...] = w_ref[...].astype(BF16)


def _cast_layer_weight(w, l, br=256, bc=4096):
    _, R, C = w.shape
    br, bc = min(br, R), min(bc, C)
    assert R % br == 0 and C % bc == 0
    return pl.pallas_call(
        _cast_kernel,
        grid=(R // br, C // bc),
        in_specs=[pl.BlockSpec((None, br, bc), lambda i, j: (l, i, j))],
        out_specs=pl.BlockSpec((br, bc), lambda i, j: (i, j)),
        out_shape=jax.ShapeDtypeStruct((R, C), BF16),
        compiler_params=pltpu.CompilerParams(
            dimension_semantics=("parallel", "parallel"),
            vmem_limit_bytes=_vmem_limit(2 * br * bc * (4 + 2) + br * bc * 4)),
        name="cast_weight",
    )(w)


def kernel(x, pre_norm, post_norm, w_in, conv_w, conv_b, lru_wa, lru_ba, lru_wx, lru_bx,
           lru_lambda, diff_lambda, diff_subln, w_branch_a, w_branch_b, w_branch_c, w_out):
    B, S, D = x.shape
    depth = w_in.shape[0]
    T = B * S
    tabs = _rope_tables(S)
    intra, kdec, qdec = _decay_tables()
    x2 = x.reshape(T, D)
    for l in range(depth):
        lam_init = 0.8 - 0.6 * math.exp(-0.3 * l)
        proj = _in_proj(x2, pre_norm[l][None, :], _cast_layer_weight(w_in, l), tabs, S)
        ya, yb = _lru_retention(
            proj, conv_w[l], conv_b[l][None, :], _gate_weights(lru_wa[l], lru_wx[l]),
            lru_ba[l][None, :], lru_bx[l][None, :], lru_lambda[l][None, :], intra, kdec, qdec, B, S)
        yc = _diff_attn(proj, diff_lambda[l], diff_subln[l][:, None], lam_init, B, S)
        m = _merge(ya, yb, yc, w_branch_a[l].astype(BF16), w_branch_b[l].astype(BF16),
                   w_branch_c[l].astype(BF16), proj)
        x2 = _out_proj(m, w_out[l].astype(BF16), x2, post_norm[l][None, :])
    return x2.reshape(B, S, D)
```
